```python
import math
import jax, jax.numpy as jnp
from jax import lax
import numpy as np

D_MODEL = 1024
BATCH = 8
SEQ = 2048
DEPTH = 2

CHUNK = 64
Q_BLOCK = 128
N_MIXERS = 2
N_HEADS = 8
QK_DIM = 64
V_DIM = 2 * QK_DIM
QKV_COLS = N_HEADS * (4 * QK_DIM + V_DIM)
CONV_WIDTH = 3
D_FF = 2816
NORM_EPS = 1e-6
SUBLN_EPS = 1e-5
N_ATTN_LAYERS = (DEPTH + 1) // 2
N_CONV_LAYERS = DEPTH // 2

kernel_name = "chunk_causal_diffattn_shortconv_hybrid"


def rms_norm(x, g, eps=NORM_EPS):
    xf = x.astype(jnp.float32)
    y = xf * lax.rsqrt(jnp.mean(xf * xf, axis=-1, keepdims=True) + eps)
    return (y * g.astype(jnp.float32)).astype(x.dtype)


def causal_dwconv(x, w):
    k = w.shape[0]
    s = x.shape[1]
    xp = jnp.pad(x, ((0, 0), (k - 1, 0), (0, 0)))
    out = xp[:, 0:s, :] * w[0]
    for j in range(1, k):
        out = out + xp[:, j:j + s, :] * w[j]
    return out


def alibi_slopes(n_heads):
    return jnp.exp2(-8.0 * jnp.arange(1, n_heads + 1, dtype=jnp.float32) / n_heads)


def diff_attention(h, w_qkv, w_o, lq1, lk1, lq2, lk2, subln_g, lambda_init):
    b, s, _ = h.shape
    proj = h @ w_qkv
    qk_w = N_HEADS * QK_DIM
    q1, q2, k1, k2, v = jnp.split(proj, [qk_w, 2 * qk_w, 3 * qk_w, 4 * qk_w], axis=-1)
    scale = QK_DIM ** -0.5
    q1 = q1.reshape(b, s, N_HEADS, QK_DIM) * scale
    q2 = q2.reshape(b, s, N_HEADS, QK_DIM) * scale
    k1 = k1.reshape(b, s, N_HEADS, QK_DIM)
    k2 = k2.reshape(b, s, N_HEADS, QK_DIM)
    v = v.reshape(b, s, N_HEADS, V_DIM)
    lam = (jnp.exp(jnp.sum(lq1.astype(jnp.float32) * lk1.astype(jnp.float32)))
           - jnp.exp(jnp.sum(lq2.astype(jnp.float32) * lk2.astype(jnp.float32)))
           + lambda_init)
    slopes = alibi_slopes(N_HEADS)
    pos = jnp.arange(s, dtype=jnp.int32)
    outs = []
    for qb in range(s // Q_BLOCK):
        q0 = qb * Q_BLOCK
        kend = q0 + Q_BLOCK
        qp = pos[q0:kend]
        kp = pos[:kend]
        allowed = (kp[None, :] // CHUNK) <= (qp[:, None] // CHUNK)
        dist = jnp.abs(qp[:, None] - kp[None, :]).astype(jnp.float32)
        bias = jnp.where(allowed[None], -slopes[:, None, None] * dist[None], -jnp.inf)
        s1 = jnp.einsum('bqhd,bkhd->bhqk', q1[:, q0:kend], k1[:, :kend]).astype(jnp.float32) + bias
        s2 = jnp.einsum('bqhd,bkhd->bhqk', q2[:, q0:kend], k2[:, :kend]).astype(jnp.float32) + bias
        a = jax.nn.softmax(s1, axis=-1) - lam * jax.nn.softmax(s2, axis=-1)
        outs.append(jnp.einsum('bhqk,bkhe->bqhe', a.astype(v.dtype), v[:, :kend]))
    o = jnp.concatenate(outs, axis=1)
    o = rms_norm(o, subln_g, SUBLN_EPS) * (1.0 - lambda_init)
    return o.reshape(b, s, N_HEADS * V_DIM) @ w_o


def short_conv_mixer(h, w_in, conv_w, w_out):
    b_gate, c_gate, hv = jnp.split(h @ w_in, 3, axis=-1)
    return (b_gate * causal_dwconv(c_gate * hv, conv_w)) @ w_out


def conv_glu_ffn(h, w_up, conv_w, w_down):
    u = causal_dwconv(h @ w_up, conv_w)
    gate, val = jnp.split(u, 2, axis=-1)
    return (jax.nn.silu(gate) * val) @ w_down


def setup_inputs(seed: int = 0) -> dict:
    key = jax.random.key(seed)
    ks = jax.random.split(key, 16)
    f32 = jnp.float32
    d = D_MODEL

    def nrm(k, shape, scale):
        return jax.random.normal(k, shape, f32) * scale

    return {
        "x": jax.random.normal(ks[0], (BATCH, SEQ, d), f32),
        "norm_g": 1.0 + nrm(ks[1], (DEPTH, 4, d), 0.05),
        "attn_w_qkv": nrm(ks[2], (N_ATTN_LAYERS, d, QKV_COLS), d ** -0.5),
        "attn_w_o": nrm(ks[3], (N_ATTN_LAYERS, N_HEADS * V_DIM, d), (N_HEADS * V_DIM) ** -0.5),
        "attn_lambda_q1": nrm(ks[4], (N_ATTN_LAYERS, QK_DIM), 0.1),
        "attn_lambda_k1": nrm(ks[5], (N_ATTN_LAYERS, QK_DIM), 0.1),
        "attn_lambda_q2": nrm(ks[6], (N_ATTN_LAYERS, QK_DIM), 0.1),
        "attn_lambda_k2": nrm(ks[7], (N_ATTN_LAYERS, QK_DIM), 0.1),
        "attn_subln_g": 1.0 + nrm(ks[8], (N_ATTN_LAYERS, V_DIM), 0.05),
        "conv_w_in": nrm(ks[9], (N_CONV_LAYERS, d, 3 * d), d ** -0.5),
        "conv_w": nrm(ks[10], (N_CONV_LAYERS, CONV_WIDTH, d), CONV_WIDTH ** -0.5),
        "conv_w_out": nrm(ks[11], (N_CONV_LAYERS, d, d), d ** -0.5),
        "ffn_w_up": nrm(ks[12], (DEPTH, d, 2 * D_FF), d ** -0.5),
        "ffn_conv_w": nrm(ks[13], (DEPTH, CONV_WIDTH, 2 * D_FF), CONV_WIDTH ** -0.5),
        "ffn_w_down": nrm(ks[14], (DEPTH, D_FF, d), D_FF ** -0.5),
    }


def reference(x, norm_g, attn_w_qkv, attn_w_o, attn_lambda_q1, attn_lambda_k1,
              attn_lambda_q2, attn_lambda_k2, attn_subln_g, conv_w_in, conv_w,
              conv_w_out, ffn_w_up, ffn_conv_w, ffn_w_down):
    for layer in range(DEPTH):
        g = norm_g[layer]
        h = rms_norm(x, g[0])
        i = layer // N_MIXERS
        if layer % N_MIXERS == 0:
            lambda_init = 0.8 - 0.6 * math.exp(-0.3 * layer)
            m = diff_attention(h, attn_w_qkv[i], attn_w_o[i], attn_lambda_q1[i],
                               attn_lambda_k1[i], attn_lambda_q2[i], attn_lambda_k2[i],
                               attn_subln_g[i], lambda_init)
        else:
            m = short_conv_mixer(h, conv_w_in[i], conv_w[i], conv_w_out[i])
        x = x + rms_norm(m, g[1])
        h = rms_norm(x, g[2])
        x = x + rms_norm(conv_glu_ffn(h, ffn_w_up[layer], ffn_conv_w[layer], ffn_w_down[layer]), g[3])
    return x
```

```python
import functools
import math

import jax
import jax.numpy as jnp
from jax import lax
from jax.experimental import pallas as pl
from jax.experimental.pallas import tpu as pltpu

D_MODEL = 1024
SEQ = 2048
CHUNK = 64
N_HEADS = 8
QK_DIM = 64
V_DIM = 128
QKV_COLS = N_HEADS * (4 * QK_DIM + V_DIM)
CONV_WIDTH = 3
D_FF = 2816
NORM_EPS = 1e-6
SUBLN_EPS = 1e-5

F32 = jnp.float32
BF16 = jnp.bfloat16

ROW_TILE = 512
ATT_TILE = 256
FFN_CHUNK = 256
MIX_CHUNK = 256
HALO = 8
VMEM_LIMIT = 56 * 1024 * 1024
MASK_VALUE = -1e30


def _rms(x, g, eps):
    ms = jnp.mean(x * x, axis=-1, keepdims=True)
    return x * lax.rsqrt(ms + eps) * g


def _resident(shape):
    nd = len(shape)
    return pl.BlockSpec(shape, lambda *_: (0,) * nd, pipeline_mode=pl.Buffered(1))


def _norm_proj_kernel(x_ref, g_ref, w_ref, o_ref, *, col_tile):
    h = _rms(x_ref[...], g_ref[...], NORM_EPS).astype(BF16)
    n = w_ref.shape[1]
    for j in range(n // col_tile):
        cs = slice(j * col_tile, (j + 1) * col_tile)
        o_ref[:, cs] = jnp.dot(h, w_ref[:, cs], preferred_element_type=F32).astype(o_ref.dtype)


def _norm_proj(x2, g, w):
    n_rows, d = x2.shape
    n = w.shape[1]
    return pl.pallas_call(
        functools.partial(_norm_proj_kernel, col_tile=512),
        grid=(n_rows // ROW_TILE,),
        in_specs=[
            pl.BlockSpec((ROW_TILE, d), lambda i: (i, 0)),
            _resident((1, d)),
            _resident((d, n)),
        ],
        out_specs=pl.BlockSpec((ROW_TILE, n), lambda i: (i, 0)),
        out_shape=jax.ShapeDtypeStruct((n_rows, n), BF16),
        compiler_params=pltpu.CompilerParams(
            dimension_semantics=("arbitrary",), vmem_limit_bytes=VMEM_LIMIT),
        name="norm_proj",
    )(x2, g.reshape(1, d), w)


def _attn_kernel(slope_ref, lq1_ref, lk1_ref, lq2_ref, lk2_ref, g_ref,
                 q_ref, k_ref, v_ref, o_ref, qs_sc, m_sc, l_sc, acc_sc, *, lambda_init):
    t = ATT_TILE
    head = pl.program_id(1)
    slope = slope_ref[head]

    lam = (jnp.exp(jnp.sum(lq1_ref[...] * lk1_ref[...], axis=-1, keepdims=True))
           - jnp.exp(jnp.sum(lq2_ref[...] * lk2_ref[...], axis=-1, keepdims=True))
           + lambda_init)

    lane = lax.broadcasted_iota(jnp.int32, (t, 2 * QK_DIM), 1)
    col = lax.broadcasted_iota(jnp.int32, (1, t), 1)
    row_i = lax.broadcasted_iota(jnp.int32, (t, t), 0)
    col_i = lax.broadcasted_iota(jnp.int32, (t, t), 1)
    allowed = (col_i // CHUNK) <= (row_i // CHUNK)
    dist = jnp.abs(row_i - col_i).astype(F32)
    diag_bias = jnp.where(allowed, slope * (row_i.astype(F32) - dist), MASK_VALUE)
    diag_bias = jnp.concatenate([diag_bias, diag_bias], axis=0)

    def update(s, v_tile):
        m_prev = m_sc[...]
        m_new = jnp.maximum(m_prev, jnp.max(s, axis=-1, keepdims=True))
        alpha = jnp.exp(m_prev - m_new)
        p = jnp.exp(s - m_new)
        l_sc[...] = alpha * l_sc[...] + jnp.sum(p, axis=-1, keepdims=True)
        acc_sc[...] = alpha * acc_sc[...] + jnp.dot(
            p.astype(BF16), v_tile, preferred_element_type=F32)
        m_sc[...] = m_new

    def scores(k_tile):
        return lax.dot_general(qs_sc[...], k_tile, (((1,), (1,)), ((), ())),
                               preferred_element_type=F32)

    def q_block(qi, carry):
        q0 = pl.multiple_of(qi * t, t)
        qq = q_ref[pl.ds(q0, t), :] * (QK_DIM ** -0.5)
        zero = jnp.zeros_like(qq)
        qs_sc[0:t, :] = jnp.where(lane < QK_DIM, qq, zero)
        qs_sc[t:2 * t, :] = jnp.where(lane >= QK_DIM, qq, zero)
        m_sc[...] = jnp.full(m_sc.shape, MASK_VALUE, F32)
        l_sc[...] = jnp.zeros(l_sc.shape, F32)
        acc_sc[...] = jnp.zeros(acc_sc.shape, F32)

        def k_block(kj, c):
            k0 = pl.multiple_of(kj * t, t)
            col_bias = slope * (col - (q0 - k0)).astype(F32)
            update(scores(k_ref[pl.ds(k0, t), :]) + col_bias, v_ref[pl.ds(k0, t), :])
            return c

        lax.fori_loop(0, qi, k_block, 0)
        update(scores(k_ref[pl.ds(q0, t), :]) + diag_bias, v_ref[pl.ds(q0, t), :])

        o_all = acc_sc[...] / l_sc[...]
        o = o_all[0:t] - lam * o_all[t:2 * t]
        o = _rms(o, g_ref[...], SUBLN_EPS) * (1.0 - lambda_init)
        o_ref[pl.ds(q0, t), :] = o.astype(o_ref.dtype)
        return carry

    lax.fori_loop(0, SEQ // t, q_block, 0)


def _attention(proj, slopes, lq1, lk1, lq2, lk2, subln_g, lambda_init):
    b = proj.shape[0]
    t = ATT_TILE
    smem = pl.BlockSpec(memory_space=pltpu.SMEM)
    lam_spec = _resident((1, QK_DIM))
    return pl.pallas_call(
        functools.partial(_attn_kernel, lambda_init=lambda_init),
        grid=(b, N_HEADS),
        in_specs=[
            smem, lam_spec, lam_spec, lam_spec, lam_spec, _resident((1, V_DIM)),
            pl.BlockSpec((None, SEQ, 2 * QK_DIM), lambda bi, h: (bi, 0, 3 * h)),
            pl.BlockSpec((None, SEQ, 2 * QK_DIM), lambda bi, h: (bi, 0, 3 * h + 1)),
            pl.BlockSpec((None, SEQ, V_DIM), lambda bi, h: (bi, 0, 3 * h + 2)),
        ],
        out_specs=pl.BlockSpec((None, SEQ, V_DIM), lambda bi, h: (bi, 0, h)),
        out_shape=jax.ShapeDtypeStruct((b, SEQ, N_HEADS * V_DIM), BF16),
        scratch_shapes=[
            pltpu.VMEM((2 * t, 2 * QK_DIM), BF16),
            pltpu.VMEM((2 * t, 1), F32),
            pltpu.VMEM((2 * t, 1), F32),
            pltpu.VMEM((2 * t, V_DIM), F32),
        ],
        compiler_params=pltpu.CompilerParams(
            dimension_semantics=("arbitrary", "arbitrary"), vmem_limit_bytes=VMEM_LIMIT),
        name="diff_attention",
    )(slopes, lq1.reshape(1, QK_DIM), lk1.reshape(1, QK_DIM), lq2.reshape(1, QK_DIM),
      lk2.reshape(1, QK_DIM), subln_g.reshape(1, V_DIM), proj, proj, proj)


def _proj_norm_res_kernel(a_ref, w_ref, x_ref, g_ref, o_ref):
    m = jnp.dot(a_ref[...], w_ref[...], preferred_element_type=F32)
    o_ref[...] = x_ref[...] + _rms(m, g_ref[...], NORM_EPS)


def _proj_norm_res(a2, w, x2, g):
    n_rows, k = a2.shape
    d = w.shape[1]
    return pl.pallas_call(
        _proj_norm_res_kernel,
        grid=(n_rows // ROW_TILE,),
        in_specs=[
            pl.BlockSpec((ROW_TILE, k), lambda i: (i, 0)),
            _resident((k, d)),
            pl.BlockSpec((ROW_TILE, d), lambda i: (i, 0)),
            _resident((1, d)),
        ],
        out_specs=pl.BlockSpec((ROW_TILE, d), lambda i: (i, 0)),
        out_shape=jax.ShapeDtypeStruct((n_rows, d), F32),
        compiler_params=pltpu.CompilerParams(
            dimension_semantics=("arbitrary",), vmem_limit_bytes=VMEM_LIMIT),
        name="proj_norm_res",
    )(a2, w, x2, g.reshape(1, d))


def _causal_conv3(u_sc, halo_ref, u, cw, first, rows):
    hist = halo_ref[...]
    u_sc[0:HALO, :] = jnp.where(first, jnp.zeros_like(hist), hist)
    u_sc[HALO:HALO + rows, :] = u
    halo_ref[...] = u[rows - HALO:rows, :]
    return (u * cw[2:3, :]
            + u_sc[HALO - 1:HALO - 1 + rows, :] * cw[1:2, :]
            + u_sc[HALO - 2:HALO - 2 + rows, :] * cw[0:1, :])


def _ffn_kernel(x_ref, gpre_ref, gpost_ref, wup_ref, cw_ref, wdn_ref, o_ref,
                h_sc, u_sc, halo_sc, act_sc):
    rows = x_ref.shape[0]
    c = FFN_CHUNK
    first = (pl.program_id(0) % (SEQ // rows)) == 0
    x = x_ref[...]
    h_sc[...] = _rms(x, gpre_ref[...], NORM_EPS).astype(BF16)
    for j in range(D_FF // c):
        cs = slice(2 * c * j, 2 * c * (j + 1))
        u = jnp.dot(h_sc[...], wup_ref[:, cs], preferred_element_type=F32)
        y = _causal_conv3(u_sc, halo_sc.at[j], u, cw_ref[:, cs], first, rows)
        gate = y[:, 0:c]
        act_sc[:, c * j:c * (j + 1)] = (gate * jax.nn.sigmoid(gate) * y[:, c:2 * c]).astype(BF16)
    m = jnp.dot(act_sc[...], wdn_ref[...], preferred_element_type=F32)
    o_ref[...] = x + _rms(m, gpost_ref[...], NORM_EPS)


def _ffn(x2, g_pre, g_post, w_up, conv_w, w_down):
    n_rows, d = x2.shape
    c = FFN_CHUNK
    n_chunks = D_FF // c
    return pl.pallas_call(
        _ffn_kernel,
        grid=(n_rows // ROW_TILE,),
        in_specs=[
            pl.BlockSpec((ROW_TILE, d), lambda i: (i, 0)),
            _resident((1, d)),
            _resident((1, d)),
            _resident((d, 2 * D_FF)),
            _resident((CONV_WIDTH, 2 * D_FF)),
            _resident((D_FF, d)),
        ],
        out_specs=pl.BlockSpec((ROW_TILE, d), lambda i: (i, 0)),
        out_shape=jax.ShapeDtypeStruct((n_rows, d), F32),
        scratch_shapes=[
            pltpu.VMEM((ROW_TILE, d), BF16),
            pltpu.VMEM((HALO + ROW_TILE, 2 * c), F32),
            pltpu.VMEM((n_chunks, HALO, 2 * c), F32),
            pltpu.VMEM((ROW_TILE, D_FF), BF16),
        ],
        compiler_params=pltpu.CompilerParams(
            dimension_semantics=("arbitrary",), vmem_limit_bytes=VMEM_LIMIT),
        name="conv_glu_ffn",
    )(x2, g_pre.reshape(1, d), g_post.reshape(1, d), w_up, conv_w, w_down)


def _mixer_kernel(x_ref, gpre_ref, gpost_ref, win_ref, cw_ref, wout_ref, o_ref,
                  h_sc, u_sc, halo_sc, y_sc):
    rows = x_ref.shape[0]
    c = MIX_CHUNK
    first = (pl.program_id(0) % (SEQ // rows)) == 0
    x = x_ref[...]
    h_sc[...] = _rms(x, gpre_ref[...], NORM_EPS).astype(BF16)
    for j in range(D_MODEL // c):
        p = jnp.dot(h_sc[...], win_ref[:, 3 * c * j:3 * c * (j + 1)],
                    preferred_element_type=F32)
        t = p[:, c:2 * c] * p[:, 2 * c:3 * c]
        y = _causal_conv3(u_sc, halo_sc.at[j], t, cw_ref[:, c * j:c * (j + 1)], first, rows)
        y_sc[:, c * j:c * (j + 1)] = (p[:, 0:c] * y).astype(BF16)
    m = jnp.dot(y_sc[...], wout_ref[...], preferred_element_type=F32)
    o_ref[...] = x + _rms(m, gpost_ref[...], NORM_EPS)


def _mixer(x2, g_pre, g_post, w_in, conv_w, w_out):
    n_rows, d = x2.shape
    c = MIX_CHUNK
    return pl.pallas_call(
        _mixer_kernel,
        grid=(n_rows // ROW_TILE,),
        in_specs=[
            pl.BlockSpec((ROW_TILE, d), lambda i: (i, 0)),
            _resident((1, d)),
            _resident((1, d)),
            _resident((d, 3 * d)),
            _resident((CONV_WIDTH, d)),
            _resident((d, d)),
        ],
        out_specs=pl.BlockSpec((ROW_TILE, d), lambda i: (i, 0)),
        out_shape=jax.ShapeDtypeStruct((n_rows, d), F32),
        scratch_shapes=[
            pltpu.VMEM((ROW_TILE, d), BF16),
            pltpu.VMEM((HALO + ROW_TILE, c), F32),
            pltpu.VMEM((d // c, HALO, c), F32),
            pltpu.VMEM((ROW_TILE, d), BF16),
        ],
        compiler_params=pltpu.CompilerParams(
            dimension_semantics=("arbitrary",), vmem_limit_bytes=VMEM_LIMIT),
        name="conv_mixer",
    )(x2, g_pre.reshape(1, d), g_post.reshape(1, d), w_in, conv_w, w_out)


def _qkv_layout(w):
    d = w.shape[0]
    qk = N_HEADS * QK_DIM
    q1, q2, k1, k2 = (w[:, i * qk:(i + 1) * qk].reshape(d, N_HEADS, QK_DIM) for i in range(4))
    v = w[:, 4 * qk:].reshape(d, N_HEADS, V_DIM)
    return jnp.concatenate([q1, q2, k1, k2, v], axis=-1).reshape(d, QKV_COLS).astype(BF16)


def _glu_layout(w, chunk):
    lead = w.shape[0]
    g = w[:, :D_FF].reshape(lead, D_FF // chunk, chunk)
    v = w[:, D_FF:].reshape(lead, D_FF // chunk, chunk)
    return jnp.concatenate([g, v], axis=-1).reshape(lead, 2 * D_FF)


def _mixer_layout(w, chunk):
    d = w.shape[0]
    parts = [w[:, i * D_MODEL:(i + 1) * D_MODEL].reshape(d, D_MODEL // chunk, chunk) for i in range(3)]
    return jnp.concatenate(parts, axis=-1).reshape(d, 3 * D_MODEL)


def kernel(x, norm_g, attn_w_qkv, attn_w_o, attn_lambda_q1, attn_lambda_k1, attn_lambda_q2,
           attn_lambda_k2, attn_subln_g, conv_w_in, conv_w, conv_w_out, ffn_w_up, ffn_conv_w,
           ffn_w_down):
    b, s, d = x.shape
    depth = norm_g.shape[0]
    slopes = jnp.exp2(-8.0 * jnp.arange(1, N_HEADS + 1, dtype=F32) / N_HEADS)
    x2 = x.reshape(b * s, d)
    for layer in range(depth):
        g = norm_g[layer]
        i = layer // 2
        if layer % 2 == 0:
            lambda_init = 0.8 - 0.6 * math.exp(-0.3 * layer)
            proj = _norm_proj(x2, g[0], _qkv_layout(attn_w_qkv[i]))
            o = _attention(proj.reshape(b, s, QKV_COLS), slopes, attn_lambda_q1[i],
                           attn_lambda_k1[i], attn_lambda_q2[i], attn_lambda_k2[i],
                           attn_subln_g[i], lambda_init)
            x2 = _proj_norm_res(o.reshape(b * s, d), attn_w_o[i].astype(BF16), x2, g[1])
        else:
            x2 = _mixer(x2, g[0], g[1],
                        _mixer_layout(conv_w_in[i], MIX_CHUNK).astype(BF16),
                        conv_w[i], conv_w_out[i].astype(BF16))
        x2 = _ffn(x2, g[2], g[3],
                  _glu_layout(ffn_w_up[layer], FFN_CHUNK).astype(BF16),
                  _glu_layout(ffn_conv_w[layer], FFN_CHUNK),
                  ffn_w_down[layer].astype(BF16))
    return x2.reshape(b, s, d)
```

```python
import functools
import math

import jax
import jax.numpy as jnp
from jax import lax
from jax.experimental import pallas as pl
from jax.experimental.pallas import tpu as pltpu

D_MODEL = 1024
SEQ = 2048
CHUNK = 64
N_HEADS = 8
QK_DIM = 64
V_DIM = 128
QK_COLS = N_HEADS * 4 * QK_DIM
CONV_WIDTH = 3
D_FF = 2816
NORM_EPS = 1e-6
SUBLN_EPS = 1e-5

F32 = jnp.float32
BF16 = jnp.bfloat16

ROW_TILE = 512
ATT_TILE = 256
ATT_HEADS = 2
FFN_CHUNK = 256
MIX_CHUNK = 256
HALO = 8
VMEM_LIMIT = 56 * 1024 * 1024
MASK_VALUE = -1e30

_NT = (((1,), (1,)), ((), ()))


def _rms(x, g, eps):
    ms = jnp.mean(x * x, axis=-1, keepdims=True)
    return x * lax.rsqrt(ms + eps) * g


def _resident(shape):
    nd = len(shape)
    return pl.BlockSpec(shape, lambda *_: (0,) * nd, pipeline_mode=pl.Buffered(1))


def _qkv_proj_kernel(x_ref, g_ref, wqk_ref, wvt_ref, qk_ref, vt_ref, *, col_tile):
    h = _rms(x_ref[...], g_ref[...], NORM_EPS).astype(BF16)
    for j in range(QK_COLS // col_tile):
        cs = slice(j * col_tile, (j + 1) * col_tile)
        qk_ref[:, cs] = jnp.dot(h, wqk_ref[:, cs], preferred_element_type=F32).astype(BF16)
    vt_ref[...] = lax.dot_general(wvt_ref[...], h, _NT, preferred_element_type=F32).astype(BF16)


def _qkv_proj(x3, g, w_qk, w_vt):
    b, s, d = x3.shape
    tiles = s // ROW_TILE
    return pl.pallas_call(
        functools.partial(_qkv_proj_kernel, col_tile=512),
        grid=(b, tiles),
        in_specs=[
            pl.BlockSpec((None, ROW_TILE, d), lambda bi, i: (bi, i, 0)),
            _resident((1, d)),
            _resident((d, QK_COLS)),
            _resident((N_HEADS * V_DIM, d)),
        ],
        out_specs=[
            pl.BlockSpec((None, ROW_TILE, QK_COLS), lambda bi, i: (bi, i, 0)),
            pl.BlockSpec((None, N_HEADS * V_DIM, ROW_TILE), lambda bi, i: (bi, 0, i)),
        ],
        out_shape=[
            jax.ShapeDtypeStruct((b, s, QK_COLS), BF16),
            jax.ShapeDtypeStruct((b, N_HEADS * V_DIM, s), BF16),
        ],
        compiler_params=pltpu.CompilerParams(
            dimension_semantics=("arbitrary", "arbitrary"), vmem_limit_bytes=VMEM_LIMIT),
        name="qkv_proj",
    )(x3, g.reshape(1, d), w_qk, w_vt)


def _attn_kernel(slope_ref, lq1_ref, lk1_ref, lq2_ref, lk2_ref, g_ref, qk_ref, vt_ref, o_ref,
                 qs_sc, kb_sc, db_sc, acc_sc, *, lambda_init, heads):
    t = ATT_TILE
    hg = pl.program_id(1)

    lam = (jnp.exp(jnp.sum(lq1_ref[...] * lk1_ref[...], axis=-1, keepdims=True))
           - jnp.exp(jnp.sum(lq2_ref[...] * lk2_ref[...], axis=-1, keepdims=True))
           + lambda_init)

    lane = lax.broadcasted_iota(jnp.int32, (t, 2 * QK_DIM), 1)
    key_i = lax.broadcasted_iota(jnp.int32, (t, 2 * t), 0)
    qry_i = lax.broadcasted_iota(jnp.int32, (t, 2 * t), 1) & (t - 1)
    allowed = (key_i // CHUNK) <= (qry_i // CHUNK)
    key_f = key_i.astype(F32)
    qry_f = qry_i.astype(F32)
    for g in range(heads):
        slope = slope_ref[hg * heads + g]
        kb_sc[g] = slope * key_f
        db_sc[g] = jnp.where(allowed, slope * (qry_f - jnp.abs(qry_f - key_f)), MASK_VALUE)

    def tile(g, k0, bias, shift, m, l):
        k_tile = qk_ref[pl.ds(k0, t), (4 * g + 2) * QK_DIM:(4 * g + 4) * QK_DIM]
        s = lax.dot_general(k_tile, qs_sc[g], _NT, preferred_element_type=F32) + bias
        m_new = jnp.maximum(m, jnp.max(s, axis=0, keepdims=True) - shift)
        alpha = jnp.exp(m - m_new)
        p = jnp.exp(s - (m_new + shift))
        l_new = alpha * l + jnp.sum(p, axis=0, keepdims=True)
        v_t = vt_ref[g * V_DIM:(g + 1) * V_DIM, pl.ds(k0, t)]
        acc_sc[g] = alpha * acc_sc[g] + jnp.dot(v_t, p.astype(BF16), preferred_element_type=F32)
        return m_new, l_new

    def q_block(qi, carry):
        q0 = pl.multiple_of(qi * t, t)
        for g in range(heads):
            qq = qk_ref[pl.ds(q0, t), 4 * g * QK_DIM:(4 * g + 2) * QK_DIM] * (QK_DIM ** -0.5)
            zero = jnp.zeros_like(qq)
            qs_sc[g, 0:t, :] = jnp.where(lane < QK_DIM, qq, zero)
            qs_sc[g, t:2 * t, :] = jnp.where(lane >= QK_DIM, qq, zero)
            acc_sc[g] = jnp.zeros(acc_sc.shape[1:], F32)

        def k_block(kj, state):
            k0 = pl.multiple_of(kj * t, t)
            dq = jnp.full((1, 2 * t), qi - kj, jnp.int32).astype(F32) * float(t)
            out = []
            for g in range(heads):
                shift = slope_ref[hg * heads + g] * dq
                out.extend(tile(g, k0, kb_sc[g], shift, state[2 * g], state[2 * g + 1]))
            return tuple(out)

        init = (jnp.full((1, 2 * t), MASK_VALUE, F32), jnp.zeros((1, 2 * t), F32)) * heads
        state = lax.fori_loop(0, qi, k_block, init)

        for g in range(heads):
            zero_shift = jnp.zeros((1, 2 * t), F32)
            _, l = tile(g, q0, db_sc[g], zero_shift, state[2 * g], state[2 * g + 1])
            o_all = acc_sc[g] / l
            o_t = o_all[:, 0:t] - lam * o_all[:, t:2 * t]
            ms = jnp.mean(o_t * o_t, axis=0, keepdims=True)
            o = (o_t * lax.rsqrt(ms + SUBLN_EPS)).T
            o = o * g_ref[...] * (1.0 - lambda_init)
            o_ref[pl.ds(q0, t), g * V_DIM:(g + 1) * V_DIM] = o.astype(o_ref.dtype)
        return carry

    lax.fori_loop(0, SEQ // t, q_block, 0)


def _attention(qk, vt, slopes, lq1, lk1, lq2, lk2, subln_g, lambda_init):
    b = qk.shape[0]
    t = ATT_TILE
    hd = ATT_HEADS
    smem = pl.BlockSpec(memory_space=pltpu.SMEM)
    lam_spec = _resident((1, QK_DIM))
    return pl.pallas_call(
        functools.partial(_attn_kernel, lambda_init=lambda_init, heads=hd),
        grid=(b, N_HEADS // hd),
        in_specs=[
            smem, lam_spec, lam_spec, lam_spec, lam_spec, _resident((1, V_DIM)),
            pl.BlockSpec((None, SEQ, hd * 4 * QK_DIM), lambda bi, h: (bi, 0, h)),
            pl.BlockSpec((None, hd * V_DIM, SEQ), lambda bi, h: (bi, h, 0)),
        ],
        out_specs=pl.BlockSpec((None, SEQ, hd * V_DIM), lambda bi, h: (bi, 0, h)),
        out_shape=jax.ShapeDtypeStruct((b, SEQ, N_HEADS * V_DIM), BF16),
        scratch_shapes=[
            pltpu.VMEM((hd, 2 * t, 2 * QK_DIM), BF16),
            pltpu.VMEM((hd, t, 2 * t), F32),
            pltpu.VMEM((hd, t, 2 * t), F32),
            pltpu.VMEM((hd, V_DIM, 2 * t), F32),
        ],
        compiler_params=pltpu.CompilerParams(
            dimension_semantics=("arbitrary", "arbitrary"), vmem_limit_bytes=VMEM_LIMIT),
        name="diff_attention",
    )(slopes, lq1.reshape(1, QK_DIM), lk1.reshape(1, QK_DIM), lq2.reshape(1, QK_DIM),
      lk2.reshape(1, QK_DIM), subln_g.reshape(1, V_DIM), qk, vt)


def _proj_norm_res_kernel(a_ref, w_ref, x_ref, g_ref, o_ref):
    m = jnp.dot(a_ref[...], w_ref[...], preferred_element_type=F32)
    o_ref[...] = x_ref[...] + _rms(m, g_ref[...], NORM_EPS)


def _proj_norm_res(a2, w, x2, g):
    n_rows, k = a2.shape
    d = w.shape[1]
    return pl.pallas_call(
        _proj_norm_res_kernel,
        grid=(n_rows // ROW_TILE,),
        in_specs=[
            pl.BlockSpec((ROW_TILE, k), lambda i: (i, 0)),
            _resident((k, d)),
            pl.BlockSpec((ROW_TILE, d), lambda i: (i, 0)),
            _resident((1, d)),
        ],
        out_specs=pl.BlockSpec((ROW_TILE, d), lambda i: (i, 0)),
        out_shape=jax.ShapeDtypeStruct((n_rows, d), F32),
        compiler_params=pltpu.CompilerParams(
            dimension_semantics=("arbitrary",), vmem_limit_bytes=VMEM_LIMIT),
        name="proj_norm_res",
    )(a2, w, x2, g.reshape(1, d))


def _causal_conv3(u_sc, halo_ref, u, cw, first, rows):
    hist = halo_ref[...]
    u_sc[0:HALO, :] = jnp.where(first, jnp.zeros_like(hist), hist)
    u_sc[HALO:HALO + rows, :] = u
    halo_ref[...] = u[rows - HALO:rows, :]
    return (u * cw[2:3, :]
            + u_sc[HALO - 1:HALO - 1 + rows, :] * cw[1:2, :]
            + u_sc[HALO - 2:HALO - 2 + rows, :] * cw[0:1, :])


def _ffn_kernel(x_ref, gpre_ref, gpost_ref, wup_ref, cw_ref, wdn_ref, o_ref,
                h_sc, u_sc, halo_sc, act_sc):
    rows = x_ref.shape[0]
    c = FFN_CHUNK
    first = (pl.program_id(0) % (SEQ // rows)) == 0
    x = x_ref[...]
    h_sc[...] = _rms(x, gpre_ref[...], NORM_EPS).astype(BF16)
    for j in range(D_FF // c):
        cs = slice(2 * c * j, 2 * c * (j + 1))
        u = jnp.dot(h_sc[...], wup_ref[:, cs], preferred_element_type=F32)
        y = _causal_conv3(u_sc, halo_sc.at[j], u, cw_ref[:, cs], first, rows)
        gate = y[:, 0:c]
        act_sc[:, c * j:c * (j + 1)] = (gate * jax.nn.sigmoid(gate) * y[:, c:2 * c]).astype(BF16)
    m = jnp.dot(act_sc[...], wdn_ref[...], preferred_element_type=F32)
    o_ref[...] = x + _rms(m, gpost_ref[...], NORM_EPS)


def _ffn(x2, g_pre, g_post, w_up, conv_w, w_down):
    n_rows, d = x2.shape
    c = FFN_CHUNK
    n_chunks = D_FF // c
    return pl.pallas_call(
        _ffn_kernel,
        grid=(n_rows // ROW_TILE,),
        in_specs=[
            pl.BlockSpec((ROW_TILE, d), lambda i: (i, 0)),
            _resident((1, d)),
            _resident((1, d)),
            _resident((d, 2 * D_FF)),
            _resident((CONV_WIDTH, 2 * D_FF)),
            _resident((D_FF, d)),
        ],
        out_specs=pl.BlockSpec((ROW_TILE, d), lambda i: (i, 0)),
        out_shape=jax.ShapeDtypeStruct((n_rows, d), F32),
        scratch_shapes=[
            pltpu.VMEM((ROW_TILE, d), BF16),
            pltpu.VMEM((HALO + ROW_TILE, 2 * c), F32),
            pltpu.VMEM((n_chunks, HALO, 2 * c), F32),
            pltpu.VMEM((ROW_TILE, D_FF), BF16),
        ],
        compiler_params=pltpu.CompilerParams(
            dimension_semantics=("arbitrary",), vmem_limit_bytes=VMEM_LIMIT),
        name="conv_glu_ffn",
    )(x2, g_pre.reshape(1, d), g_post.reshape(1, d), w_up, conv_w, w_down)


def _mixer_kernel(x_ref, gpre_ref, gpost_ref, win_ref, cw_ref, wout_ref, o_ref,
                  h_sc, u_sc, halo_sc, y_sc):
    rows = x_ref.shape[0]
    c = MIX_CHUNK
    first = (pl.program_id(0) % (SEQ // rows)) == 0
    x = x_ref[...]
    h_sc[...] = _rms(x, gpre_ref[...], NORM_EPS).astype(BF16)
    for j in range(D_MODEL // c):
        p = jnp.dot(h_sc[...], win_ref[:, 3 * c * j:3 * c * (j + 1)],
                    preferred_element_type=F32)
        t = p[:, c:2 * c] * p[:, 2 * c:3 * c]
        y = _causal_conv3(u_sc, halo_sc.at[j], t, cw_ref[:, c * j:c * (j + 1)], first, rows)
        y_sc[:, c * j:c * (j + 1)] = (p[:, 0:c] * y).astype(BF16)
    m = jnp.dot(y_sc[...], wout_ref[...], preferred_element_type=F32)
    o_ref[...] = x + _rms(m, gpost_ref[...], NORM_EPS)


def _mixer(x2, g_pre, g_post, w_in, conv_w, w_out):
    n_rows, d = x2.shape
    c = MIX_CHUNK
    return pl.pallas_call(
        _mixer_kernel,
        grid=(n_rows // ROW_TILE,),
        in_specs=[
            pl.BlockSpec((ROW_TILE, d), lambda i: (i, 0)),
            _resident((1, d)),
            _resident((1, d)),
            _resident((d, 3 * d)),
            _resident((CONV_WIDTH, d)),
            _resident((d, d)),
        ],
        out_specs=pl.BlockSpec((ROW_TILE, d), lambda i: (i, 0)),
        out_shape=jax.ShapeDtypeStruct((n_rows, d), F32),
        scratch_shapes=[
            pltpu.VMEM((ROW_TILE, d), BF16),
            pltpu.VMEM((HALO + ROW_TILE, c), F32),
            pltpu.VMEM((d // c, HALO, c), F32),
            pltpu.VMEM((ROW_TILE, d), BF16),
        ],
        compiler_params=pltpu.CompilerParams(
            dimension_semantics=("arbitrary",), vmem_limit_bytes=VMEM_LIMIT),
        name="conv_mixer",
    )(x2, g_pre.reshape(1, d), g_post.reshape(1, d), w_in, conv_w, w_out)


def _qkv_layout(w):
    d = w.shape[0]
    qk = N_HEADS * QK_DIM
    parts = [w[:, i * qk:(i + 1) * qk].reshape(d, N_HEADS, QK_DIM) for i in range(4)]
    w_qk = jnp.concatenate(parts, axis=-1).reshape(d, QK_COLS).astype(BF16)
    w_vt = w[:, 4 * qk:].T.astype(BF16)
    return w_qk, w_vt


def _glu_layout(w, chunk):
    lead = w.shape[0]
    g = w[:, :D_FF].reshape(lead, D_FF // chunk, chunk)
    v = w[:, D_FF:].reshape(lead, D_FF // chunk, chunk)
    return jnp.concatenate([g, v], axis=-1).reshape(lead, 2 * D_FF)


def _mixer_layout(w, chunk):
    d = w.shape[0]
    parts = [w[:, i * D_MODEL:(i + 1) * D_MODEL].reshape(d, D_MODEL // chunk, chunk) for i in range(3)]
    return jnp.concatenate(parts, axis=-1).reshape(d, 3 * D_MODEL)


def kernel(x, norm_g, attn_w_qkv, attn_w_o, attn_lambda_q1, attn_lambda_k1, attn_lambda_q2,
           attn_lambda_k2, attn_subln_g, conv_w_in, conv_w, conv_w_out, ffn_w_up, ffn_conv_w,
           ffn_w_down):
    b, s, d = x.shape
    depth = norm_g.shape[0]
    slopes = jnp.exp2(-8.0 * jnp.arange(1, N_HEADS + 1, dtype=F32) / N_HEADS)
    x2 = x.reshape(b * s, d)
    for layer in range(depth):
        g = norm_g[layer]
        i = layer // 2
        if layer % 2 == 0:
            lambda_init = 0.8 - 0.6 * math.exp(-0.3 * layer)
            w_qk, w_vt = _qkv_layout(attn_w_qkv[i])
            qk, vt = _qkv_proj(x2.reshape(b, s, d), g[0], w_qk, w_vt)
            o = _attention(qk, vt, slopes, attn_lambda_q1[i], attn_lambda_k1[i],
                           attn_lambda_q2[i], attn_lambda_k2[i], attn_subln_g[i], lambda_init)
            x2 = _proj_norm_res(o.reshape(b * s, d), attn_w_o[i].astype(BF16), x2, g[1])
        else:
            x2 = _mixer(x2, g[0], g[1],
                        _mixer_layout(conv_w_in[i], MIX_CHUNK).astype(BF16),
                        conv_w[i], conv_w_out[i].astype(BF16))
        x2 = _ffn(x2, g[2], g[3],
                  _glu_layout(ffn_w_up[layer], FFN_CHUNK).astype(BF16),
                  _glu_layout(ffn_conv_w[layer], FFN_CHUNK),
                  ffn_w_down[layer].astype(BF16))
    return x2.reshape(b, s, d)
```

```python
import functools
import math

import jax
import jax.numpy as jnp
from jax import lax
from jax.experimental import pallas as pl
from jax.experimental.pallas import tpu as pltpu

D_MODEL = 1024
SEQ = 2048
CHUNK = 64
N_HEADS = 8
QK_DIM = 64
V_DIM = 128
QK_COLS = N_HEADS * 4 * QK_DIM
CONV_WIDTH = 3
D_FF = 2816
NORM_EPS = 1e-6
SUBLN_EPS = 1e-5

F32 = jnp.float32
BF16 = jnp.bfloat16

ROW_TILE = 512
ATT_TILE = 256
SCORE_LEAD = 2
S_SLOTS = SCORE_LEAD + 1
FFN_CHUNK = 256
MIX_CHUNK = 256
HALO = 8
VMEM_LIMIT = 56 * 1024 * 1024
MASK_VALUE = -1e30

_NT = (((1,), (1,)), ((), ()))


def _rms(x, g, eps):
    ms = jnp.mean(x * x, axis=-1, keepdims=True)
    return x * lax.rsqrt(ms + eps) * g


def _resident(shape):
    nd = len(shape)
    return pl.BlockSpec(shape, lambda *_: (0,) * nd, pipeline_mode=pl.Buffered(1))


def _qkv_proj_kernel(x_ref, g_ref, wqk_ref, wvt_ref, qk_ref, vt_ref, *, col_tile):
    h = _rms(x_ref[...], g_ref[...], NORM_EPS).astype(BF16)
    for j in range(QK_COLS // col_tile):
        cs = slice(j * col_tile, (j + 1) * col_tile)
        qk_ref[:, cs] = jnp.dot(h, wqk_ref[:, cs], preferred_element_type=F32).astype(BF16)
    vt_ref[...] = lax.dot_general(wvt_ref[...], h, _NT, preferred_element_type=F32).astype(BF16)


def _qkv_proj(x3, g, w_qk, w_vt):
    b, s, d = x3.shape
    tiles = s // ROW_TILE
    return pl.pallas_call(
        functools.partial(_qkv_proj_kernel, col_tile=512),
        grid=(b, tiles),
        in_specs=[
            pl.BlockSpec((None, ROW_TILE, d), lambda bi, i: (bi, i, 0)),
            _resident((1, d)),
            _resident((d, QK_COLS)),
            _resident((N_HEADS * V_DIM, d)),
        ],
        out_specs=[
            pl.BlockSpec((None, ROW_TILE, QK_COLS), lambda bi, i: (bi, i, 0)),
            pl.BlockSpec((None, N_HEADS * V_DIM, ROW_TILE), lambda bi, i: (bi, 0, i)),
        ],
        out_shape=[
            jax.ShapeDtypeStruct((b, s, QK_COLS), BF16),
            jax.ShapeDtypeStruct((b, N_HEADS * V_DIM, s), BF16),
        ],
        compiler_params=pltpu.CompilerParams(
            dimension_semantics=("arbitrary", "arbitrary"), vmem_limit_bytes=VMEM_LIMIT),
        name="qkv_proj",
    )(x3, g.reshape(1, d), w_qk, w_vt)


def _attn_kernel(slope_ref, lq1_ref, lk1_ref, lq2_ref, lk2_ref, g_ref, qk_ref, vt_ref, o_ref,
                 qs_sc, kb_sc, db_sc, acc_sc, s_sc, *, lambda_init):
    t = ATT_TILE
    nq = SEQ // t
    slope = slope_ref[pl.program_id(1)]

    lam = (jnp.exp(jnp.sum(lq1_ref[...] * lk1_ref[...], axis=-1, keepdims=True))
           - jnp.exp(jnp.sum(lq2_ref[...] * lk2_ref[...], axis=-1, keepdims=True))
           + lambda_init)

    lane = lax.broadcasted_iota(jnp.int32, (t, 2 * QK_DIM), 1)
    key_i = lax.broadcasted_iota(jnp.int32, (t, 2 * t), 0)
    qry_i = lax.broadcasted_iota(jnp.int32, (t, 2 * t), 1) & (t - 1)
    allowed = (key_i // CHUNK) <= (qry_i // CHUNK)
    key_f = key_i.astype(F32)
    qry_f = qry_i.astype(F32)
    kb_sc[...] = slope * key_f
    db_sc[...] = jnp.where(allowed, slope * (qry_f - jnp.abs(qry_f - key_f)), MASK_VALUE)

    for qi in range(nq):
        qq = qk_ref[qi * t:(qi + 1) * t, 0:2 * QK_DIM] * (QK_DIM ** -0.5)
        zero = jnp.zeros_like(qq)
        qs_sc[qi, 0:t, :] = jnp.where(lane < QK_DIM, qq, zero)
        qs_sc[qi, t:2 * t, :] = jnp.where(lane >= QK_DIM, qq, zero)

    tiles = [(kj, qi) for kj in range(nq) for qi in range(kj, nq)]

    def emit_scores(idx):
        kj, qi = tiles[idx]
        k_tile = qk_ref[kj * t:(kj + 1) * t, 2 * QK_DIM:4 * QK_DIM]
        bias = db_sc[...] if qi == kj else kb_sc[...]
        s_sc[idx % S_SLOTS] = lax.dot_general(
            k_tile, qs_sc[qi], _NT, preferred_element_type=F32) + bias

    for idx in range(min(SCORE_LEAD, len(tiles))):
        emit_scores(idx)

    m = [None] * nq
    l = [None] * nq
    for idx, (kj, qi) in enumerate(tiles):
        if idx + SCORE_LEAD < len(tiles):
            emit_scores(idx + SCORE_LEAD)
        s = s_sc[idx % S_SLOTS]
        v_t = vt_ref[:, kj * t:(kj + 1) * t]
        shift = slope * jnp.full((1, 2 * t), float((qi - kj) * t), F32)
        m_tile = jnp.max(s, axis=0, keepdims=True) - shift
        if kj == 0:
            m[qi] = m_tile
            p = jnp.exp(s - (m[qi] + shift))
            l[qi] = jnp.sum(p, axis=0, keepdims=True)
            acc_sc[qi] = jnp.dot(v_t, p.astype(BF16), preferred_element_type=F32)
        else:
            m_new = jnp.maximum(m[qi], m_tile)
            alpha = jnp.exp(m[qi] - m_new)
            p = jnp.exp(s - (m_new + shift))
            l[qi] = alpha * l[qi] + jnp.sum(p, axis=0, keepdims=True)
            acc_sc[qi] = alpha * acc_sc[qi] + jnp.dot(
                v_t, p.astype(BF16), preferred_element_type=F32)
            m[qi] = m_new
        if qi == kj:
            o_all = acc_sc[qi] / l[qi]
            o_t = o_all[:, 0:t] - lam * o_all[:, t:2 * t]
            ms = jnp.mean(o_t * o_t, axis=0, keepdims=True)
            o = (o_t * lax.rsqrt(ms + SUBLN_EPS)).T
            o = o * g_ref[...] * (1.0 - lambda_init)
            o_ref[qi * t:(qi + 1) * t, :] = o.astype(o_ref.dtype)


def _attention(qk, vt, slopes, lq1, lk1, lq2, lk2, subln_g, lambda_init):
    b = qk.shape[0]
    t = ATT_TILE
    smem = pl.BlockSpec(memory_space=pltpu.SMEM)
    lam_spec = _resident((1, QK_DIM))
    return pl.pallas_call(
        functools.partial(_attn_kernel, lambda_init=lambda_init),
        grid=(b, N_HEADS),
        in_specs=[
            smem, lam_spec, lam_spec, lam_spec, lam_spec, _resident((1, V_DIM)),
            pl.BlockSpec((None, SEQ, 4 * QK_DIM), lambda bi, h: (bi, 0, h)),
            pl.BlockSpec((None, V_DIM, SEQ), lambda bi, h: (bi, h, 0)),
        ],
        out_specs=pl.BlockSpec((None, SEQ, V_DIM), lambda bi, h: (bi, 0, h)),
        out_shape=jax.ShapeDtypeStruct((b, SEQ, N_HEADS * V_DIM), BF16),
        scratch_shapes=[
            pltpu.VMEM((SEQ // t, 2 * t, 2 * QK_DIM), BF16),
            pltpu.VMEM((t, 2 * t), F32),
            pltpu.VMEM((t, 2 * t), F32),
            pltpu.VMEM((SEQ // t, V_DIM, 2 * t), F32),
            pltpu.VMEM((S_SLOTS, t, 2 * t), F32),
        ],
        compiler_params=pltpu.CompilerParams(
            dimension_semantics=("arbitrary", "arbitrary"), vmem_limit_bytes=VMEM_LIMIT),
        name="diff_attention",
    )(slopes, lq1.reshape(1, QK_DIM), lk1.reshape(1, QK_DIM), lq2.reshape(1, QK_DIM),
      lk2.reshape(1, QK_DIM), subln_g.reshape(1, V_DIM), qk, vt)


def _proj_norm_res_kernel(a_ref, w_ref, x_ref, g_ref, o_ref):
    m = jnp.dot(a_ref[...], w_ref[...], preferred_element_type=F32)
    o_ref[...] = x_ref[...] + _rms(m, g_ref[...], NORM_EPS)


def _proj_norm_res(a2, w, x2, g):
    n_rows, k = a2.shape
    d = w.shape[1]
    return pl.pallas_call(
        _proj_norm_res_kernel,
        grid=(n_rows // ROW_TILE,),
        in_specs=[
            pl.BlockSpec((ROW_TILE, k), lambda i: (i, 0)),
            _resident((k, d)),
            pl.BlockSpec((ROW_TILE, d), lambda i: (i, 0)),
            _resident((1, d)),
        ],
        out_specs=pl.BlockSpec((ROW_TILE, d), lambda i: (i, 0)),
        out_shape=jax.ShapeDtypeStruct((n_rows, d), F32),
        compiler_params=pltpu.CompilerParams(
            dimension_semantics=("arbitrary",), vmem_limit_bytes=VMEM_LIMIT),
        name="proj_norm_res",
    )(a2, w, x2, g.reshape(1, d))


def _causal_conv3(u_sc, halo_ref, u, cw, first, rows):
    hist = halo_ref[...]
    u_sc[0:HALO, :] = jnp.where(first, jnp.zeros_like(hist), hist)
    u_sc[HALO:HALO + rows, :] = u
    halo_ref[...] = u[rows - HALO:rows, :]
    return (u * cw[2:3, :]
            + u_sc[HALO - 1:HALO - 1 + rows, :] * cw[1:2, :]
            + u_sc[HALO - 2:HALO - 2 + rows, :] * cw[0:1, :])


def _ffn_kernel(x_ref, gpre_ref, gpost_ref, wup_ref, cw_ref, wdn_ref, o_ref,
                h_sc, u_sc, halo_sc, act_sc):
    rows = x_ref.shape[0]
    c = FFN_CHUNK
    first = (pl.program_id(0) % (SEQ // rows)) == 0
    x = x_ref[...]
    h_sc[...] = _rms(x, gpre_ref[...], NORM_EPS).astype(BF16)
    for j in range(D_FF // c):
        cs = slice(2 * c * j, 2 * c * (j + 1))
        u = jnp.dot(h_sc[...], wup_ref[:, cs], preferred_element_type=F32)
        y = _causal_conv3(u_sc, halo_sc.at[j], u, cw_ref[:, cs], first, rows)
        gate = y[:, 0:c]
        act_sc[:, c * j:c * (j + 1)] = (gate * jax.nn.sigmoid(gate) * y[:, c:2 * c]).astype(BF16)
    m = jnp.dot(act_sc[...], wdn_ref[...], preferred_element_type=F32)
    o_ref[...] = x + _rms(m, gpost_ref[...], NORM_EPS)


def _ffn(x2, g_pre, g_post, w_up, conv_w, w_down):
    n_rows, d = x2.shape
    c = FFN_CHUNK
    n_chunks = D_FF // c
    return pl.pallas_call(
        _ffn_kernel,
        grid=(n_rows // ROW_TILE,),
        in_specs=[
            pl.BlockSpec((ROW_TILE, d), lambda i: (i, 0)),
            _resident((1, d)),
            _resident((1, d)),
            _resident((d, 2 * D_FF)),
            _resident((CONV_WIDTH, 2 * D_FF)),
            _resident((D_FF, d)),
        ],
        out_specs=pl.BlockSpec((ROW_TILE, d), lambda i: (i, 0)),
        out_shape=jax.ShapeDtypeStruct((n_rows, d), F32),
        scratch_shapes=[
            pltpu.VMEM((ROW_TILE, d), BF16),
            pltpu.VMEM((HALO + ROW_TILE, 2 * c), F32),
            pltpu.VMEM((n_chunks, HALO, 2 * c), F32),
            pltpu.VMEM((ROW_TILE, D_FF), BF16),
        ],
        compiler_params=pltpu.CompilerParams(
            dimension_semantics=("arbitrary",), vmem_limit_bytes=VMEM_LIMIT),
        name="conv_glu_ffn",
    )(x2, g_pre.reshape(1, d), g_post.reshape(1, d), w_up, conv_w, w_down)


def _mixer_kernel(x_ref, gpre_ref, gpost_ref, win_ref, cw_ref, wout_ref, o_ref,
                  h_sc, u_sc, halo_sc, y_sc):
    rows = x_ref.shape[0]
    c = MIX_CHUNK
    first = (pl.program_id(0) % (SEQ // rows)) == 0
    x = x_ref[...]
    h_sc[...] = _rms(x, gpre_ref[...], NORM_EPS).astype(BF16)
    for j in range(D_MODEL // c):
        p = jnp.dot(h_sc[...], win_ref[:, 3 * c * j:3 * c * (j + 1)],
                    preferred_element_type=F32)
        t = p[:, c:2 * c] * p[:, 2 * c:3 * c]
        y = _causal_conv3(u_sc, halo_sc.at[j], t, cw_ref[:, c * j:c * (j + 1)], first, rows)
        y_sc[:, c * j:c * (j + 1)] = (p[:, 0:c] * y).astype(BF16)
    m = jnp.dot(y_sc[...], wout_ref[...], preferred_element_type=F32)
    o_ref[...] = x + _rms(m, gpost_ref[...], NORM_EPS)


def _mixer(x2, g_pre, g_post, w_in, conv_w, w_out):
    n_rows, d = x2.shape
    c = MIX_CHUNK
    return pl.pallas_call(
        _mixer_kernel,
        grid=(n_rows // ROW_TILE,),
        in_specs=[
            pl.BlockSpec((ROW_TILE, d), lambda i: (i, 0)),
            _resident((1, d)),
            _resident((1, d)),
            _resident((d, 3 * d)),
            _resident((CONV_WIDTH, d)),
            _resident((d, d)),
        ],
        out_specs=pl.BlockSpec((ROW_TILE, d), lambda i: (i, 0)),
        out_shape=jax.ShapeDtypeStruct((n_rows, d), F32),
        scratch_shapes=[
            pltpu.VMEM((ROW_TILE, d), BF16),
            pltpu.VMEM((HALO + ROW_TILE, c), F32),
            pltpu.VMEM((d // c, HALO, c), F32),
            pltpu.VMEM((ROW_TILE, d), BF16),
        ],
        compiler_params=pltpu.CompilerParams(
            dimension_semantics=("arbitrary",), vmem_limit_bytes=VMEM_LIMIT),
        name="conv_mixer",
    )(x2, g_pre.reshape(1, d), g_post.reshape(1, d), w_in, conv_w, w_out)


def _qkv_layout(w):
    d = w.shape[0]
    qk = N_HEADS * QK_DIM
    parts = [w[:, i * qk:(i + 1) * qk].reshape(d, N_HEADS, QK_DIM) for i in range(4)]
    w_qk = jnp.concatenate(parts, axis=-1).reshape(d, QK_COLS).astype(BF16)
    w_vt = w[:, 4 * qk:].T.astype(BF16)
    return w_qk, w_vt


def _glu_layout(w, chunk):
    lead = w.shape[0]
    g = w[:, :D_FF].reshape(lead, D_FF // chunk, chunk)
    v = w[:, D_FF:].reshape(lead, D_FF // chunk, chunk)
    return jnp.concatenate([g, v], axis=-1).reshape(lead, 2 * D_FF)


def _mixer_layout(w, chunk):
    d = w.shape[0]
    parts = [w[:, i * D_MODEL:(i + 1) * D_MODEL].reshape(d, D_MODEL // chunk, chunk) for i in range(3)]
    return jnp.concatenate(parts, axis=-1).reshape(d, 3 * D_MODEL)


def kernel(x, norm_g, attn_w_qkv, attn_w_o, attn_lambda_q1, attn_lambda_k1, attn_lambda_q2,
           attn_lambda_k2, attn_subln_g, conv_w_in, conv_w, conv_w_out, ffn_w_up, ffn_conv_w,
           ffn_w_down):
    b, s, d = x.shape
    depth = norm_g.shape[0]
    slopes = jnp.exp2(-8.0 * jnp.arange(1, N_HEADS + 1, dtype=F32) / N_HEADS)
    x2 = x.reshape(b * s, d)
    for layer in range(depth):
        g = norm_g[layer]
        i = layer // 2
        if layer % 2 == 0:
            lambda_init = 0.8 - 0.6 * math.exp(-0.3 * layer)
            w_qk, w_vt = _qkv_layout(attn_w_qkv[i])
            qk, vt = _qkv_proj(x2.reshape(b, s, d), g[0], w_qk, w_vt)
            o = _attention(qk, vt, slopes, attn_lambda_q1[i], attn_lambda_k1[i],
                           attn_lambda_q2[i], attn_lambda_k2[i], attn_subln_g[i], lambda_init)
            x2 = _proj_norm_res(o.reshape(b * s, d), attn_w_o[i].astype(BF16), x2, g[1])
        else:
            x2 = _mixer(x2, g[0], g[1],
                        _mixer_layout(conv_w_in[i], MIX_CHUNK).astype(BF16),
                        conv_w[i], conv_w_out[i].astype(BF16))
        x2 = _ffn(x2, g[2], g[3],
                  _glu_layout(ffn_w_up[layer], FFN_CHUNK).astype(BF16),
                  _glu_layout(ffn_conv_w[layer], FFN_CHUNK),
                  ffn_w_down[layer].astype(BF16))
    return x2.reshape(b, s, d)
```

```python
import functools
import math

import jax
import jax.numpy as jnp
from jax import lax
from jax.experimental import pallas as pl
from jax.experimental.pallas import tpu as pltpu

D_MODEL = 1024
SEQ = 2048
CHUNK = 64
N_HEADS = 8
QK_DIM = 64
V_DIM = 128
QK_COLS = N_HEADS * 4 * QK_DIM
CONV_WIDTH = 3
D_FF = 2816
NORM_EPS = 1e-6
SUBLN_EPS = 1e-5

F32 = jnp.float32
BF16 = jnp.bfloat16

ROW_TILE = 512
ATT_TILE = 256
SCORE_LEAD = 2
S_SLOTS = SCORE_LEAD + 1
FFN_CHUNK = 256
MIX_CHUNK = 256
HALO = 8
VMEM_LIMIT = 56 * 1024 * 1024
MASK_VALUE = -1e30
LOG2E = math.log2(math.e)

_NT = (((1,), (1,)), ((), ()))


def _rms(x, g, eps):
    ms = jnp.mean(x * x, axis=-1, keepdims=True)
    return x * lax.rsqrt(ms + eps) * g


def _resident(shape):
    nd = len(shape)
    return pl.BlockSpec(shape, lambda *_: (0,) * nd, pipeline_mode=pl.Buffered(1))


def _qkv_proj_kernel(x_ref, g_ref, wqk_ref, scale_ref, wvt_ref, qk_ref, vt_ref, *, col_tile):
    h = _rms(x_ref[...], g_ref[...], NORM_EPS).astype(BF16)
    for j in range(QK_COLS // col_tile):
        cs = slice(j * col_tile, (j + 1) * col_tile)
        qk = jnp.dot(h, wqk_ref[:, cs], preferred_element_type=F32) * scale_ref[:, cs]
        qk_ref[:, cs] = qk.astype(BF16)
    vt_ref[...] = lax.dot_general(wvt_ref[...], h, _NT, preferred_element_type=F32).astype(BF16)


def _qkv_proj(x3, g, w_qk, w_vt):
    b, s, d = x3.shape
    tiles = s // ROW_TILE
    is_query = (jnp.arange(QK_COLS) % (4 * QK_DIM)) < 2 * QK_DIM
    col_scale = jnp.where(is_query, LOG2E * QK_DIM ** -0.5, 1.0).astype(F32).reshape(1, QK_COLS)
    return pl.pallas_call(
        functools.partial(_qkv_proj_kernel, col_tile=512),
        grid=(b, tiles),
        in_specs=[
            pl.BlockSpec((None, ROW_TILE, d), lambda bi, i: (bi, i, 0)),
            _resident((1, d)),
            _resident((d, QK_COLS)),
            _resident((1, QK_COLS)),
            _resident((N_HEADS * V_DIM, d)),
        ],
        out_specs=[
            pl.BlockSpec((None, ROW_TILE, QK_COLS), lambda bi, i: (bi, i, 0)),
            pl.BlockSpec((None, N_HEADS * V_DIM, ROW_TILE), lambda bi, i: (bi, 0, i)),
        ],
        out_shape=[
            jax.ShapeDtypeStruct((b, s, QK_COLS), BF16),
            jax.ShapeDtypeStruct((b, N_HEADS * V_DIM, s), BF16),
        ],
        compiler_params=pltpu.CompilerParams(
            dimension_semantics=("arbitrary", "arbitrary"), vmem_limit_bytes=VMEM_LIMIT),
        name="qkv_proj",
    )(x3, g.reshape(1, d), w_qk, col_scale, w_vt)


def _attn_kernel(slope_ref, lq1_ref, lk1_ref, lq2_ref, lk2_ref, g_ref, qk_ref, vt_ref, o_ref,
                 qs_sc, kb_sc, db_sc, acc_sc, s_sc, *, lambda_init):
    t = ATT_TILE
    nq = SEQ // t
    slope = slope_ref[pl.program_id(1)] * LOG2E

    lam =(jnp.exp(jnp.sum(lq1_ref[...] * lk1_ref[...], axis=-1, keepdims=True))
           - jnp.exp(jnp.sum(lq2_ref[...] * lk2_ref[...], axis=-1, keepdims=True))
           + lambda_init)

    lane = lax.broadcasted_iota(jnp.int32, (t, 2 * QK_DIM), 1)
    key_i = lax.broadcasted_iota(jnp.int32, (t, 2 * t), 0)
    qry_i = lax.broadcasted_iota(jnp.int32, (t, 2 * t), 1) & (t - 1)
    allowed = (key_i // CHUNK) <= (qry_i // CHUNK)
    key_f = key_i.astype(F32)
    qry_f = qry_i.astype(F32)
    kb_sc[...] = slope * key_f
    db_sc[...] = jnp.where(allowed, slope * (qry_f - jnp.abs(qry_f - key_f)), MASK_VALUE)

    for qi in range(nq):
        qq = qk_ref[qi * t:(qi + 1) * t, 0:2 * QK_DIM]
        zero = jnp.zeros_like(qq)
        qs_sc[qi, 0:t, :] = jnp.where(lane < QK_DIM, qq, zero)
        qs_sc[qi, t:2 * t, :] = jnp.where(lane >= QK_DIM, qq, zero)

    tiles = [(kj, qi) for kj in range(nq) for qi in range(kj, nq)]

    def emit_scores(idx):
        kj, qi = tiles[idx]
        k_tile = qk_ref[kj * t:(kj + 1) * t, 2 * QK_DIM:4 * QK_DIM]
        bias = db_sc[...] if qi == kj else kb_sc[...]
        s = lax.dot_general(k_tile, qs_sc[qi], _NT, preferred_element_type=F32) + bias
        s_sc[idx % S_SLOTS] = s
        s_max[idx] = jnp.max(s, axis=0, keepdims=True)

    s_max = [None] * len(tiles)
    for idx in range(min(SCORE_LEAD, len(tiles))):
        emit_scores(idx)

    m = [None] * nq
    l = [None] * nq
    for idx, (kj, qi) in enumerate(tiles):
        if idx + SCORE_LEAD < len(tiles):
            emit_scores(idx + SCORE_LEAD)
        s = s_sc[idx % S_SLOTS]
        v_t = vt_ref[:, kj * t:(kj + 1) * t]
        shift = slope * jnp.full((1, 2 * t), float((qi - kj) * t), F32)
        m_tile = s_max[idx] - shift
        if kj == 0:
            m[qi] = m_tile
            p = jnp.exp2(s - (m[qi] + shift))
            l[qi] = jnp.sum(p, axis=0, keepdims=True)
            acc_sc[qi] = jnp.dot(v_t, p.astype(BF16), preferred_element_type=F32)
        else:
            m_new = jnp.maximum(m[qi], m_tile)
            alpha = jnp.exp2(m[qi] - m_new)
            p = jnp.exp2(s - (m_new + shift))
            l[qi] = alpha * l[qi] + jnp.sum(p, axis=0, keepdims=True)
            acc_sc[qi] = alpha * acc_sc[qi] + jnp.dot(
                v_t, p.astype(BF16), preferred_element_type=F32)
            m[qi] = m_new
        if qi == kj:
            o_all = acc_sc[qi] / l[qi]
            o_t = o_all[:, 0:t] - lam * o_all[:, t:2 * t]
            ms = jnp.mean(o_t * o_t, axis=0, keepdims=True)
            o = (o_t * lax.rsqrt(ms + SUBLN_EPS)).T
            o = o * g_ref[...] * (1.0 - lambda_init)
            o_ref[qi * t:(qi + 1) * t, :] = o.astype(o_ref.dtype)


def _attention(qk, vt, slopes, lq1, lk1, lq2, lk2, subln_g, lambda_init):
    b = qk.shape[0]
    t = ATT_TILE
    smem = pl.BlockSpec(memory_space=pltpu.SMEM)
    lam_spec = _resident((1, QK_DIM))
    return pl.pallas_call(
        functools.partial(_attn_kernel, lambda_init=lambda_init),
        grid=(b, N_HEADS),
        in_specs=[
            smem, lam_spec, lam_spec, lam_spec, lam_spec, _resident((1, V_DIM)),
            pl.BlockSpec((None, SEQ, 4 * QK_DIM), lambda bi, h: (bi, 0, h)),
            pl.BlockSpec((None, V_DIM, SEQ), lambda bi, h: (bi, h, 0)),
        ],
        out_specs=pl.BlockSpec((None, SEQ, V_DIM), lambda bi, h: (bi, 0, h)),
        out_shape=jax.ShapeDtypeStruct((b, SEQ, N_HEADS * V_DIM), BF16),
        scratch_shapes=[
            pltpu.VMEM((SEQ // t, 2 * t, 2 * QK_DIM), BF16),
            pltpu.VMEM((t, 2 * t), F32),
            pltpu.VMEM((t, 2 * t), F32),
            pltpu.VMEM((SEQ // t, V_DIM, 2 * t), F32),
            pltpu.VMEM((S_SLOTS, t, 2 * t), F32),
        ],
        compiler_params=pltpu.CompilerParams(
            dimension_semantics=("arbitrary", "arbitrary"), vmem_limit_bytes=VMEM_LIMIT),
        name="diff_attention",
    )(slopes, lq1.reshape(1, QK_DIM), lk1.reshape(1, QK_DIM), lq2.reshape(1, QK_DIM),
      lk2.reshape(1, QK_DIM), subln_g.reshape(1, V_DIM), qk, vt)


def _causal_conv3(u_sc, halo_ref, u, cw, first, rows):
    hist = halo_ref[...]
    u_sc[0:HALO, :] = jnp.where(first, jnp.zeros_like(hist), hist)
    u_sc[HALO:HALO + rows, :] = u
    halo_ref[...] = u[rows - HALO:rows, :]
    return (u * cw[2:3, :]
            + u_sc[HALO - 1:HALO - 1 + rows, :] * cw[1:2, :]
            + u_sc[HALO - 2:HALO - 2 + rows, :] * cw[0:1, :])


def _ffn_tile(x, first, gpre_ref, gpost_ref, wup_ref, cw_ref, wdn_ref,
              h_sc, u_sc, halo_sc, act_sc):
    rows = x.shape[0]
    c = FFN_CHUNK
    h_sc[...] = _rms(x, gpre_ref[...], NORM_EPS).astype(BF16)
    for j in range(D_FF // c):
        gs = slice(c * j, c * (j + 1))
        vs = slice(D_FF + c * j, D_FF + c * (j + 1))
        h = h_sc[...]
        u = jnp.concatenate(
            [jnp.dot(h, wup_ref[:, gs], preferred_element_type=F32),
             jnp.dot(h, wup_ref[:, vs], preferred_element_type=F32)], axis=1)
        cw = jnp.concatenate([cw_ref[:, gs], cw_ref[:, vs]], axis=1)
        y = _causal_conv3(u_sc, halo_sc.at[j], u, cw, first, rows)
        gate = y[:, 0:c]
        act_sc[:, gs] = (gate * jax.nn.sigmoid(gate) * y[:, c:2 * c]).astype(BF16)
    m = jnp.dot(act_sc[...], wdn_ref[...], preferred_element_type=F32)
    return x + _rms(m, gpost_ref[...], NORM_EPS)


def _ffn_kernel(x_ref, gpre_ref, gpost_ref, wup_ref, cw_ref, wdn_ref, o_ref, *scratch):
    first = (pl.program_id(0) % (SEQ // x_ref.shape[0])) == 0
    o_ref[...] = _ffn_tile(x_ref[...], first, gpre_ref, gpost_ref, wup_ref, cw_ref, wdn_ref,
                           *scratch)


def _oproj_ffn_kernel(a_ref, wo_ref, gmix_ref, x_ref, gpre_ref, gpost_ref, wup_ref, cw_ref,
                      wdn_ref, o_ref, *scratch):
    first = (pl.program_id(0) % (SEQ // x_ref.shape[0])) == 0
    mix = jnp.dot(a_ref[...], wo_ref[...], preferred_element_type=F32)
    x = x_ref[...] + _rms(mix, gmix_ref[...], NORM_EPS)
    o_ref[...] = _ffn_tile(x, first, gpre_ref, gpost_ref, wup_ref, cw_ref, wdn_ref, *scratch)


def _ffn(x2, g_pre, g_post, w_up, conv_w, w_down, attn=None):
    n_rows, d = x2.shape
    c = FFN_CHUNK
    row_spec = pl.BlockSpec((ROW_TILE, d), lambda i: (i, 0))
    in_specs = [
        row_spec,
        _resident((1, d)),
        _resident((1, d)),
        _resident((d, 2 * D_FF)),
        _resident((CONV_WIDTH, 2 * D_FF)),
        _resident((D_FF, d)),
    ]
    args = [x2, g_pre.reshape(1, d), g_post.reshape(1, d), w_up, conv_w, w_down]
    body = _ffn_kernel
    if attn is not None:
        a2, w_o, g_mix = attn
        in_specs = [pl.BlockSpec((ROW_TILE, a2.shape[1]), lambda i: (i, 0)),
                    _resident(w_o.shape), _resident((1, d))] + in_specs
        args = [a2, w_o, g_mix.reshape(1, d)] + args
        body = _oproj_ffn_kernel
    return pl.pallas_call(
        body,
        grid=(n_rows // ROW_TILE,),
        in_specs=in_specs,
        out_specs=row_spec,
        out_shape=jax.ShapeDtypeStruct((n_rows, d), F32),
        scratch_shapes=[
            pltpu.VMEM((ROW_TILE, d), BF16),
            pltpu.VMEM((HALO + ROW_TILE, 2 * c), F32),
            pltpu.VMEM((D_FF // c, HALO, 2 * c), F32),
            pltpu.VMEM((ROW_TILE, D_FF), BF16),
        ],
        compiler_params=pltpu.CompilerParams(
            dimension_semantics=("arbitrary",), vmem_limit_bytes=VMEM_LIMIT),
        name="conv_glu_ffn",
    )(*args)


def _mixer_kernel(x_ref, gpre_ref, gpost_ref, win_ref, cw_ref, wout_ref, o_ref,
                  h_sc, u_sc, halo_sc, y_sc):
    rows = x_ref.shape[0]
    c = MIX_CHUNK
    first = (pl.program_id(0) % (SEQ // rows)) == 0
    x = x_ref[...]
    h_sc[...] = _rms(x, gpre_ref[...], NORM_EPS).astype(BF16)
    for j in range(D_MODEL // c):
        cs = slice(c * j, c * (j + 1))
        h = h_sc[...]
        b_gate, c_gate, hv = (
            jnp.dot(h, win_ref[:, part * D_MODEL + c * j:part * D_MODEL + c * (j + 1)],
                    preferred_element_type=F32) for part in range(3))
        y = _causal_conv3(u_sc, halo_sc.at[j], c_gate * hv, cw_ref[:, cs], first, rows)
        y_sc[:, cs] = (b_gate * y).astype(BF16)
    m = jnp.dot(y_sc[...], wout_ref[...], preferred_element_type=F32)
    o_ref[...] = x + _rms(m, gpost_ref[...], NORM_EPS)


def _mixer(x2, g_pre, g_post, w_in, conv_w, w_out):
    n_rows, d = x2.shape
    c = MIX_CHUNK
    return pl.pallas_call(
        _mixer_kernel,
        grid=(n_rows // ROW_TILE,),
        in_specs=[
            pl.BlockSpec((ROW_TILE, d), lambda i: (i, 0)),
            _resident((1, d)),
            _resident((1, d)),
            _resident((d, 3 * d)),
            _resident((CONV_WIDTH, d)),
            _resident((d, d)),
        ],
        out_specs=pl.BlockSpec((ROW_TILE, d), lambda i: (i, 0)),
        out_shape=jax.ShapeDtypeStruct((n_rows, d), F32),
        scratch_shapes=[
            pltpu.VMEM((ROW_TILE, d), BF16),
            pltpu.VMEM((HALO + ROW_TILE, c), F32),
            pltpu.VMEM((d // c, HALO, c), F32),
            pltpu.VMEM((ROW_TILE, d), BF16),
        ],
        compiler_params=pltpu.CompilerParams(
            dimension_semantics=("arbitrary",), vmem_limit_bytes=VMEM_LIMIT),
        name="conv_mixer",
    )(x2, g_pre.reshape(1, d), g_post.reshape(1, d), w_in, conv_w, w_out)


def _qkv_layout(w):
    d = w.shape[0]
    qk = N_HEADS * QK_DIM
    parts = [w[:, i * qk:(i + 1) * qk].reshape(d, N_HEADS, QK_DIM) for i in range(4)]
    w_qk = jnp.concatenate(parts, axis=-1).reshape(d, QK_COLS).astype(BF16)
    w_vt = w[:, 4 * qk:].T.astype(BF16)
    return w_qk, w_vt


def kernel(x, norm_g, attn_w_qkv, attn_w_o, attn_lambda_q1, attn_lambda_k1, attn_lambda_q2,
           attn_lambda_k2, attn_subln_g, conv_w_in, conv_w, conv_w_out, ffn_w_up, ffn_conv_w,
           ffn_w_down):
    b, s, d = x.shape
    depth = norm_g.shape[0]
    slopes = jnp.exp2(-8.0 * jnp.arange(1, N_HEADS + 1, dtype=F32) / N_HEADS)
    x2 = x.reshape(b * s, d)
    for layer in range(depth):
        g = norm_g[layer]
        i = layer // 2
        if layer % 2 == 0:
            lambda_init = 0.8 - 0.6 * math.exp(-0.3 * layer)
            w_qk, w_vt = _qkv_layout(attn_w_qkv[i])
            qk, vt = _qkv_proj(x2.reshape(b, s, d), g[0], w_qk, w_vt)
            o = _attention(qk, vt, slopes, attn_lambda_q1[i], attn_lambda_k1[i],
                           attn_lambda_q2[i], attn_lambda_k2[i], attn_subln_g[i], lambda_init)
            attn = (o.reshape(b * s, d), attn_w_o[i].astype(BF16), g[1])
        else:
            x2 = _mixer(x2, g[0], g[1], conv_w_in[i].astype(BF16), conv_w[i],
                        conv_w_out[i].astype(BF16))
            attn = None
        x2 = _ffn(x2, g[2], g[3], ffn_w_up[layer].astype(BF16), ffn_conv_w[layer],
                  ffn_w_down[layer].astype(BF16), attn=attn)
    return x2.reshape(b, s, d)
```

```python
import functools
import math

import jax
import jax.numpy as jnp
from jax import lax
from jax.experimental import pallas as pl
from jax.experimental.pallas import tpu as pltpu

D_MODEL = 1024
SEQ = 2048
CHUNK = 64
N_HEADS = 8
QK_DIM = 64
V_DIM = 128
QK_COLS = N_HEADS * 4 * QK_DIM
CONV_WIDTH = 3
D_FF = 2816
NORM_EPS = 1e-6
SUBLN_EPS = 1e-5

F32 = jnp.float32
BF16 = jnp.bfloat16

ROW_TILE = 512
ATT_TILE = 256
SCORE_LEAD = 2
S_SLOTS = SCORE_LEAD + 1
ONES_ROWS = 16
FFN_CHUNK = 256
MIX_CHUNK = 256
HALO = 8
VMEM_LIMIT = 56 * 1024 * 1024
MASK_VALUE = -1e30
LOG2E = math.log2(math.e)

_NT = (((1,), (1,)), ((), ()))


def _rms(x, g, eps):
    ms = jnp.mean(x * x, axis=-1, keepdims=True)
    return x * lax.rsqrt(ms + eps) * g


def _resident(shape):
    nd = len(shape)
    return pl.BlockSpec(shape, lambda *_: (0,) * nd, pipeline_mode=pl.Buffered(1))


def _qkv_proj_kernel(x_ref, g_ref, wqk_ref, scale_ref, wvt_ref, qk_ref, vt_ref, *, col_tile):
    h = _rms(x_ref[...], g_ref[...], NORM_EPS).astype(BF16)
    for j in range(QK_COLS // col_tile):
        cs = slice(j * col_tile, (j + 1) * col_tile)
        qk = jnp.dot(h, wqk_ref[:, cs], preferred_element_type=F32) * scale_ref[:, cs]
        qk_ref[:, cs] = qk.astype(BF16)
    vt_ref[...] = lax.dot_general(wvt_ref[...], h, _NT, preferred_element_type=F32).astype(BF16)


def _qkv_proj(x3, g, w_qk, w_vt):
    b, s, d = x3.shape
    tiles = s // ROW_TILE
    is_query = (jnp.arange(QK_COLS) % (4 * QK_DIM)) < 2 * QK_DIM
    col_scale = jnp.where(is_query, LOG2E * QK_DIM ** -0.5, 1.0).astype(F32).reshape(1, QK_COLS)
    return pl.pallas_call(
        functools.partial(_qkv_proj_kernel, col_tile=512),
        grid=(b, tiles),
        in_specs=[
            pl.BlockSpec((None, ROW_TILE, d), lambda bi, i: (bi, i, 0)),
            _resident((1, d)),
            _resident((d, QK_COLS)),
            _resident((1, QK_COLS)),
            _resident((N_HEADS * V_DIM, d)),
        ],
        out_specs=[
            pl.BlockSpec((None, ROW_TILE, QK_COLS), lambda bi, i: (bi, i, 0)),
            pl.BlockSpec((None, N_HEADS * V_DIM, ROW_TILE), lambda bi, i: (bi, 0, i)),
        ],
        out_shape=[
            jax.ShapeDtypeStruct((b, s, QK_COLS), BF16),
            jax.ShapeDtypeStruct((b, N_HEADS * V_DIM, s), BF16),
        ],
        compiler_params=pltpu.CompilerParams(
            dimension_semantics=("arbitrary", "arbitrary"), vmem_limit_bytes=VMEM_LIMIT),
        name="qkv_proj",
    )(x3, g.reshape(1, d), w_qk, col_scale, w_vt)


def _attn_kernel(slope_ref, lq1_ref, lk1_ref, lq2_ref, lk2_ref, g_ref, qk_ref, vt_ref, o_ref,
                 qs_sc, kb_sc, db_sc, acc_sc, s_sc, *, lambda_init):
    t = ATT_TILE
    nq = SEQ // t
    slope = slope_ref[pl.program_id(1)] * LOG2E

    lam =(jnp.exp(jnp.sum(lq1_ref[...] * lk1_ref[...], axis=-1, keepdims=True))
           - jnp.exp(jnp.sum(lq2_ref[...] * lk2_ref[...], axis=-1, keepdims=True))
           + lambda_init)

    lane = lax.broadcasted_iota(jnp.int32, (t, 2 * QK_DIM), 1)
    key_i = lax.broadcasted_iota(jnp.int32, (t, 2 * t), 0)
    qry_i = lax.broadcasted_iota(jnp.int32, (t, 2 * t), 1) & (t - 1)
    allowed = (key_i // CHUNK) <= (qry_i // CHUNK)
    key_f = key_i.astype(F32)
    qry_f = qry_i.astype(F32)
    kb_sc[...] = slope * key_f
    db_sc[...] = jnp.where(allowed, slope * (qry_f - jnp.abs(qry_f - key_f)), MASK_VALUE)

    for qi in range(nq):
        qq = qk_ref[qi * t:(qi + 1) * t, 0:2 * QK_DIM]
        zero = jnp.zeros_like(qq)
        qs_sc[qi, 0:t, :] = jnp.where(lane < QK_DIM, qq, zero)
        qs_sc[qi, t:2 * t, :] = jnp.where(lane >= QK_DIM, qq, zero)

    tiles = [(kj, qi) for kj in range(nq) for qi in range(kj, nq)]

    def emit_scores(idx):
        kj, qi = tiles[idx]
        k_tile = qk_ref[kj * t:(kj + 1) * t, 2 * QK_DIM:4 * QK_DIM]
        bias = db_sc[...] if qi == kj else kb_sc[...]
        s = lax.dot_general(k_tile, qs_sc[qi], _NT, preferred_element_type=F32) + bias
        s_sc[idx % S_SLOTS] = s
        s_max[idx] = jnp.max(s, axis=0, keepdims=True)

    s_max = [None] * len(tiles)
    for idx in range(min(SCORE_LEAD, len(tiles))):
        emit_scores(idx)

    ones_rows = jnp.ones((ONES_ROWS, t), BF16)
    m = [None] * nq
    for idx, (kj, qi) in enumerate(tiles):
        if idx + SCORE_LEAD < len(tiles):
            emit_scores(idx + SCORE_LEAD)
        s = s_sc[idx % S_SLOTS]
        v_ext = jnp.concatenate([vt_ref[:, kj * t:(kj + 1) * t], ones_rows], axis=0)
        shift = slope * jnp.full((1, 2 * t), float((qi - kj) * t), F32)
        m_tile = s_max[idx] - shift
        if kj == 0:
            m[qi] = m_tile
            p = jnp.exp2((s - (m[qi] + shift)).astype(BF16))
            acc_sc[qi] = jnp.dot(v_ext, p, preferred_element_type=F32)
        else:
            m_new = jnp.maximum(m[qi], m_tile)
            alpha = jnp.exp2(m[qi] - m_new)
            p = jnp.exp2((s - (m_new + shift)).astype(BF16))
            acc_sc[qi] = alpha * acc_sc[qi] + jnp.dot(v_ext, p, preferred_element_type=F32)
            m[qi] = m_new
        if qi == kj:
            acc = acc_sc[qi]
            o_all = acc[0:V_DIM] / acc[V_DIM:V_DIM + 1]
            o_t = o_all[:, 0:t] - lam * o_all[:, t:2 * t]
            ms = jnp.mean(o_t * o_t, axis=0, keepdims=True)
            o = (o_t * lax.rsqrt(ms + SUBLN_EPS)).T
            o = o * g_ref[...] * (1.0 - lambda_init)
            o_ref[qi * t:(qi + 1) * t, :] = o.astype(o_ref.dtype)


def _attention(qk, vt, slopes, lq1, lk1, lq2, lk2, subln_g, lambda_init):
    b = qk.shape[0]
    t = ATT_TILE
    smem = pl.BlockSpec(memory_space=pltpu.SMEM)
    lam_spec = _resident((1, QK_DIM))
    return pl.pallas_call(
        functools.partial(_attn_kernel, lambda_init=lambda_init),
        grid=(b, N_HEADS),
        in_specs=[
            smem, lam_spec, lam_spec, lam_spec, lam_spec, _resident((1, V_DIM)),
            pl.BlockSpec((None, SEQ, 4 * QK_DIM), lambda bi, h: (bi, 0, h)),
            pl.BlockSpec((None, V_DIM, SEQ), lambda bi, h: (bi, h, 0)),
        ],
        out_specs=pl.BlockSpec((None, SEQ, V_DIM), lambda bi, h: (bi, 0, h)),
        out_shape=jax.ShapeDtypeStruct((b, SEQ, N_HEADS * V_DIM), BF16),
        scratch_shapes=[
            pltpu.VMEM((SEQ // t, 2 * t, 2 * QK_DIM), BF16),
            pltpu.VMEM((t, 2 * t), F32),
            pltpu.VMEM((t, 2 * t), F32),
            pltpu.VMEM((SEQ // t, V_DIM + ONES_ROWS, 2 * t), F32),
            pltpu.VMEM((S_SLOTS, t, 2 * t), F32),
        ],
        compiler_params=pltpu.CompilerParams(
            dimension_semantics=("arbitrary", "arbitrary"), vmem_limit_bytes=VMEM_LIMIT),
        name="diff_attention",
    )(slopes, lq1.reshape(1, QK_DIM), lk1.reshape(1, QK_DIM), lq2.reshape(1, QK_DIM),
      lk2.reshape(1, QK_DIM), subln_g.reshape(1, V_DIM), qk, vt)


def _causal_conv3(u_sc, halo_ref, u, cw, first, rows):
    hist = halo_ref[...]
    u_sc[0:HALO, :] = jnp.where(first, jnp.zeros_like(hist), hist)
    u_sc[HALO:HALO + rows, :] = u
    halo_ref[...] = u[rows - HALO:rows, :]
    return (u * cw[2:3, :]
            + u_sc[HALO - 1:HALO - 1 + rows, :] * cw[1:2, :]
            + u_sc[HALO - 2:HALO - 2 + rows, :] * cw[0:1, :])


def _ffn_tile(x, first, gpre_ref, gpost_ref, wup_ref, cw_ref, wdn_ref,
              h_sc, u_sc, halo_sc, act_sc):
    rows = x.shape[0]
    c = FFN_CHUNK
    h_sc[...] = _rms(x, gpre_ref[...], NORM_EPS).astype(BF16)
    for j in range(D_FF // c):
        gs = slice(c * j, c * (j + 1))
        vs = slice(D_FF + c * j, D_FF + c * (j + 1))
        h = h_sc[...]
        u = jnp.concatenate(
            [jnp.dot(h, wup_ref[:, gs], preferred_element_type=F32),
             jnp.dot(h, wup_ref[:, vs], preferred_element_type=F32)], axis=1)
        cw = jnp.concatenate([cw_ref[:, gs], cw_ref[:, vs]], axis=1)
        y = _causal_conv3(u_sc, halo_sc.at[j], u, cw, first, rows)
        gate = y[:, 0:c]
        act_sc[:, gs] = (gate * jax.nn.sigmoid(gate) * y[:, c:2 * c]).astype(BF16)
    m = jnp.dot(act_sc[...], wdn_ref[...], preferred_element_type=F32)
    return x + _rms(m, gpost_ref[...], NORM_EPS)


def _ffn_kernel(x_ref, gpre_ref, gpost_ref, wup_ref, cw_ref, wdn_ref, o_ref, *scratch):
    first = (pl.program_id(0) % (SEQ // x_ref.shape[0])) == 0
    o_ref[...] = _ffn_tile(x_ref[...], first, gpre_ref, gpost_ref, wup_ref, cw_ref, wdn_ref,
                           *scratch)


def _oproj_ffn_kernel(a_ref, wo_ref, gmix_ref, x_ref, gpre_ref, gpost_ref, wup_ref, cw_ref,
                      wdn_ref, o_ref, *scratch):
    first = (pl.program_id(0) % (SEQ // x_ref.shape[0])) == 0
    mix = jnp.dot(a_ref[...], wo_ref[...], preferred_element_type=F32)
    x = x_ref[...] + _rms(mix, gmix_ref[...], NORM_EPS)
    o_ref[...] = _ffn_tile(x, first, gpre_ref, gpost_ref, wup_ref, cw_ref, wdn_ref, *scratch)


def _ffn(x2, g_pre, g_post, w_up, conv_w, w_down, attn=None):
    n_rows, d = x2.shape
    c = FFN_CHUNK
    row_spec = pl.BlockSpec((ROW_TILE, d), lambda i: (i, 0))
    in_specs = [
        row_spec,
        _resident((1, d)),
        _resident((1, d)),
        _resident((d, 2 * D_FF)),
        _resident((CONV_WIDTH, 2 * D_FF)),
        _resident((D_FF, d)),
    ]
    args = [x2, g_pre.reshape(1, d), g_post.reshape(1, d), w_up, conv_w, w_down]
    body = _ffn_kernel
    if attn is not None:
        a2, w_o, g_mix = attn
        in_specs = [pl.BlockSpec((ROW_TILE, a2.shape[1]), lambda i: (i, 0)),
                    _resident(w_o.shape), _resident((1, d))] + in_specs
        args = [a2, w_o, g_mix.reshape(1, d)] + args
        body = _oproj_ffn_kernel
    return pl.pallas_call(
        body,
        grid=(n_rows // ROW_TILE,),
        in_specs=in_specs,
        out_specs=row_spec,
        out_shape=jax.ShapeDtypeStruct((n_rows, d), F32),
        scratch_shapes=[
            pltpu.VMEM((ROW_TILE, d), BF16),
            pltpu.VMEM((HALO + ROW_TILE, 2 * c), F32),
            pltpu.VMEM((D_FF // c, HALO, 2 * c), F32),
            pltpu.VMEM((ROW_TILE, D_FF), BF16),
        ],
        compiler_params=pltpu.CompilerParams(
            dimension_semantics=("arbitrary",), vmem_limit_bytes=VMEM_LIMIT),
        name="conv_glu_ffn",
    )(*args)


def _mixer_kernel(x_ref, gpre_ref, gpost_ref, win_ref, cw_ref, wout_ref, o_ref,
                  h_sc, u_sc, halo_sc, y_sc):
    rows = x_ref.shape[0]
    c = MIX_CHUNK
    first = (pl.program_id(0) % (SEQ // rows)) == 0
    x = x_ref[...]
    h_sc[...] = _rms(x, gpre_ref[...], NORM_EPS).astype(BF16)
    for j in range(D_MODEL // c):
        cs = slice(c * j, c * (j + 1))
        h = h_sc[...]
        b_gate, c_gate, hv = (
            jnp.dot(h, win_ref[:, part * D_MODEL + c * j:part * D_MODEL + c * (j + 1)],
                    preferred_element_type=F32) for part in range(3))
        y = _causal_conv3(u_sc, halo_sc.at[j], c_gate * hv, cw_ref[:, cs], first, rows)
        y_sc[:, cs] = (b_gate * y).astype(BF16)
    m = jnp.dot(y_sc[...], wout_ref[...], preferred_element_type=F32)
    o_ref[...] = x + _rms(m, gpost_ref[...], NORM_EPS)


def _mixer(x2, g_pre, g_post, w_in, conv_w, w_out):
    n_rows, d = x2.shape
    c = MIX_CHUNK
    return pl.pallas_call(
        _mixer_kernel,
        grid=(n_rows // ROW_TILE,),
        in_specs=[
            pl.BlockSpec((ROW_TILE, d), lambda i: (i, 0)),
            _resident((1, d)),
            _resident((1, d)),
            _resident((d, 3 * d)),
            _resident((CONV_WIDTH, d)),
            _resident((d, d)),
        ],
        out_specs=pl.BlockSpec((ROW_TILE, d), lambda i: (i, 0)),
        out_shape=jax.ShapeDtypeStruct((n_rows, d), F32),
        scratch_shapes=[
            pltpu.VMEM((ROW_TILE, d), BF16),
            pltpu.VMEM((HALO + ROW_TILE, c), F32),
            pltpu.VMEM((d // c, HALO, c), F32),
            pltpu.VMEM((ROW_TILE, d), BF16),
        ],
        compiler_params=pltpu.CompilerParams(
            dimension_semantics=("arbitrary",), vmem_limit_bytes=VMEM_LIMIT),
        name="conv_mixer",
    )(x2, g_pre.reshape(1, d), g_post.reshape(1, d), w_in, conv_w, w_out)


def _qkv_layout(w):
    d = w.shape[0]
    qk = N_HEADS * QK_DIM
    parts = [w[:, i * qk:(i + 1) * qk].reshape(d, N_HEADS, QK_DIM) for i in range(4)]
    w_qk = jnp.concatenate(parts, axis=-1).reshape(d, QK_COLS).astype(BF16)
    w_vt = w[:, 4 * qk:].T.astype(BF16)
    return w_qk, w_vt


def kernel(x, norm_g, attn_w_qkv, attn_w_o, attn_lambda_q1, attn_lambda_k1, attn_lambda_q2,
           attn_lambda_k2, attn_subln_g, conv_w_in, conv_w, conv_w_out, ffn_w_up, ffn_conv_w,
           ffn_w_down):
    b, s, d = x.shape
    depth = norm_g.shape[0]
    slopes = jnp.exp2(-8.0 * jnp.arange(1, N_HEADS + 1, dtype=F32) / N_HEADS)
    x2 = x.reshape(b * s, d)
    for layer in range(depth):
        g = norm_g[layer]
        i = layer // 2
        if layer % 2 == 0:
            lambda_init = 0.8 - 0.6 * math.exp(-0.3 * layer)
            w_qk, w_vt = _qkv_layout(attn_w_qkv[i])
            qk, vt = _qkv_proj(x2.reshape(b, s, d), g[0], w_qk, w_vt)
            o = _attention(qk, vt, slopes, attn_lambda_q1[i], attn_lambda_k1[i],
                           attn_lambda_q2[i], attn_lambda_k2[i], attn_subln_g[i], lambda_init)
            attn = (o.reshape(b * s, d), attn_w_o[i].astype(BF16), g[1])
        else:
            x2 = _mixer(x2, g[0], g[1], conv_w_in[i].astype(BF16), conv_w[i],
                        conv_w_out[i].astype(BF16))
            attn = None
        x2 = _ffn(x2, g[2], g[3], ffn_w_up[layer].astype(BF16), ffn_conv_w[layer],
                  ffn_w_down[layer].astype(BF16), attn=attn)
    return x2.reshape(b, s, d)
```

```python
import functools
import math

import jax
import jax.numpy as jnp
from jax import lax
from jax.experimental import pallas as pl
from jax.experimental.pallas import tpu as pltpu

D_MODEL = 1024
SEQ = 2048
CHUNK = 64
N_HEADS = 8
QK_DIM = 64
V_DIM = 128
QK_COLS = N_HEADS * 4 * QK_DIM
CONV_WIDTH = 3
D_FF = 2816
NORM_EPS = 1e-6
SUBLN_EPS = 1e-5

F32 = jnp.float32
BF16 = jnp.bfloat16

ROW_TILE = 512
ATT_TILE = 256
SCORE_LEAD = 2
S_SLOTS = SCORE_LEAD + 1
ONES_ROWS = 16
FFN_GROUPS = 2
FFN_CHUNK = 256
MIX_CHUNK = 256
HALO = 8
VMEM_LIMIT = 56 * 1024 * 1024
MASK_VALUE = -1e30
LOG2E = math.log2(math.e)

_NT = (((1,), (1,)), ((), ()))


def _rms(x, g, eps):
    ms = jnp.mean(x * x, axis=-1, keepdims=True)
    return x * lax.rsqrt(ms + eps) * g


def _resident(shape):
    nd = len(shape)
    return pl.BlockSpec(shape, lambda *_: (0,) * nd, pipeline_mode=pl.Buffered(1))


def _qkv_proj_kernel(x_ref, g_ref, wqk_ref, scale_ref, wvt_ref, qk_ref, vt_ref, *, col_tile):
    h = _rms(x_ref[...], g_ref[...], NORM_EPS).astype(BF16)
    for j in range(QK_COLS // col_tile):
        cs = slice(j * col_tile, (j + 1) * col_tile)
        qk = jnp.dot(h, wqk_ref[:, cs], preferred_element_type=F32) * scale_ref[:, cs]
        qk_ref[:, cs] = qk.astype(BF16)
    vt_ref[...] = lax.dot_general(wvt_ref[...], h, _NT, preferred_element_type=F32).astype(BF16)


def _qkv_proj(x3, g, w_qk, w_vt):
    b, s, d = x3.shape
    tiles = s // ROW_TILE
    is_query = (jnp.arange(QK_COLS) % (4 * QK_DIM)) < 2 * QK_DIM
    col_scale = jnp.where(is_query, LOG2E * QK_DIM ** -0.5, 1.0).astype(F32).reshape(1, QK_COLS)
    return pl.pallas_call(
        functools.partial(_qkv_proj_kernel, col_tile=512),
        grid=(b, tiles),
        in_specs=[
            pl.BlockSpec((None, ROW_TILE, d), lambda bi, i: (bi, i, 0)),
            _resident((1, d)),
            _resident((d, QK_COLS)),
            _resident((1, QK_COLS)),
            _resident((N_HEADS * V_DIM, d)),
        ],
        out_specs=[
            pl.BlockSpec((None, ROW_TILE, QK_COLS), lambda bi, i: (bi, i, 0)),
            pl.BlockSpec((None, N_HEADS * V_DIM, ROW_TILE), lambda bi, i: (bi, 0, i)),
        ],
        out_shape=[
            jax.ShapeDtypeStruct((b, s, QK_COLS), BF16),
            jax.ShapeDtypeStruct((b, N_HEADS * V_DIM, s), BF16),
        ],
        compiler_params=pltpu.CompilerParams(
            dimension_semantics=("arbitrary", "arbitrary"), vmem_limit_bytes=VMEM_LIMIT),
        name="qkv_proj",
    )(x3, g.reshape(1, d), w_qk, col_scale, w_vt)


def _attn_kernel(slope_ref, lq1_ref, lk1_ref, lq2_ref, lk2_ref, g_ref, qk_ref, vt_ref, o_ref,
                 qs_sc, kb_sc, db_sc, acc_sc, s_sc, *, lambda_init):
    t = ATT_TILE
    nq = SEQ // t
    slope = slope_ref[pl.program_id(1)] * LOG2E

    lam =(jnp.exp(jnp.sum(lq1_ref[...] * lk1_ref[...], axis=-1, keepdims=True))
           - jnp.exp(jnp.sum(lq2_ref[...] * lk2_ref[...], axis=-1, keepdims=True))
           + lambda_init)

    lane = lax.broadcasted_iota(jnp.int32, (t, 2 * QK_DIM), 1)
    key_i = lax.broadcasted_iota(jnp.int32, (t, 2 * t), 0)
    qry_i = lax.broadcasted_iota(jnp.int32, (t, 2 * t), 1) & (t - 1)
    allowed = (key_i // CHUNK) <= (qry_i // CHUNK)
    key_f = key_i.astype(F32)
    qry_f = qry_i.astype(F32)
    kb_sc[...] = slope * key_f
    db_sc[...] = jnp.where(allowed, slope * (qry_f - jnp.abs(qry_f - key_f)), MASK_VALUE)

    for qi in range(nq):
        qq = qk_ref[qi * t:(qi + 1) * t, 0:2 * QK_DIM]
        zero = jnp.zeros_like(qq)
        qs_sc[qi, 0:t, :] = jnp.where(lane < QK_DIM, qq, zero)
        qs_sc[qi, t:2 * t, :] = jnp.where(lane >= QK_DIM, qq, zero)

    tiles = [(kj, qi) for kj in range(nq) for qi in range(kj, nq)]

    def emit_scores(idx):
        kj, qi = tiles[idx]
        k_tile = qk_ref[kj * t:(kj + 1) * t, 2 * QK_DIM:4 * QK_DIM]
        bias = db_sc[...] if qi == kj else kb_sc[...]
        s = lax.dot_general(k_tile, qs_sc[qi], _NT, preferred_element_type=F32) + bias
        s_sc[idx % S_SLOTS] = s
        s_max[idx] = jnp.max(s, axis=0, keepdims=True)

    s_max = [None] * len(tiles)
    for idx in range(min(SCORE_LEAD, len(tiles))):
        emit_scores(idx)

    ones_rows = jnp.ones((ONES_ROWS, t), BF16)
    m = [None] * nq
    for idx, (kj, qi) in enumerate(tiles):
        if idx + SCORE_LEAD < len(tiles):
            emit_scores(idx + SCORE_LEAD)
        s = s_sc[idx % S_SLOTS]
        v_ext = jnp.concatenate([vt_ref[:, kj * t:(kj + 1) * t], ones_rows], axis=0)
        shift = slope * jnp.full((1, 2 * t), float((qi - kj) * t), F32)
        m_tile = s_max[idx] - shift
        if kj == 0:
            m[qi] = m_tile
            p = jnp.exp2((s - (m[qi] + shift)).astype(BF16))
            acc_sc[qi] = jnp.dot(v_ext, p, preferred_element_type=F32)
        else:
            m_new = jnp.maximum(m[qi], m_tile)
            alpha = jnp.exp2(m[qi] - m_new)
            p = jnp.exp2((s - (m_new + shift)).astype(BF16))
            acc_sc[qi] = alpha * acc_sc[qi] + jnp.dot(v_ext, p, preferred_element_type=F32)
            m[qi] = m_new
        if qi == kj:
            acc = acc_sc[qi]
            o_all = acc[0:V_DIM] / acc[V_DIM:V_DIM + 1]
            o_t = o_all[:, 0:t] - lam * o_all[:, t:2 * t]
            ms = jnp.mean(o_t * o_t, axis=0, keepdims=True)
            o = (o_t * lax.rsqrt(ms + SUBLN_EPS)).T
            o = o * g_ref[...] * (1.0 - lambda_init)
            o_ref[qi * t:(qi + 1) * t, :] = o.astype(o_ref.dtype)


def _attention(qk, vt, slopes, lq1, lk1, lq2, lk2, subln_g, lambda_init):
    b = qk.shape[0]
    t = ATT_TILE
    smem = pl.BlockSpec(memory_space=pltpu.SMEM)
    lam_spec = _resident((1, QK_DIM))
    return pl.pallas_call(
        functools.partial(_attn_kernel, lambda_init=lambda_init),
        grid=(b, N_HEADS),
        in_specs=[
            smem, lam_spec, lam_spec, lam_spec, lam_spec, _resident((1, V_DIM)),
            pl.BlockSpec((None, SEQ, 4 * QK_DIM), lambda bi, h: (bi, 0, h)),
            pl.BlockSpec((None, V_DIM, SEQ), lambda bi, h: (bi, h, 0)),
        ],
        out_specs=pl.BlockSpec((None, SEQ, V_DIM), lambda bi, h: (bi, 0, h)),
        out_shape=jax.ShapeDtypeStruct((b, SEQ, N_HEADS * V_DIM), BF16),
        scratch_shapes=[
            pltpu.VMEM((SEQ // t, 2 * t, 2 * QK_DIM), BF16),
            pltpu.VMEM((t, 2 * t), F32),
            pltpu.VMEM((t, 2 * t), F32),
            pltpu.VMEM((SEQ // t, V_DIM + ONES_ROWS, 2 * t), F32),
            pltpu.VMEM((S_SLOTS, t, 2 * t), F32),
        ],
        compiler_params=pltpu.CompilerParams(
            dimension_semantics=("arbitrary", "arbitrary"), vmem_limit_bytes=VMEM_LIMIT),
        name="diff_attention",
    )(slopes, lq1.reshape(1, QK_DIM), lk1.reshape(1, QK_DIM), lq2.reshape(1, QK_DIM),
      lk2.reshape(1, QK_DIM), subln_g.reshape(1, V_DIM), qk, vt)


def _causal_conv3(u_sc, halo_ref, u, cw, first, rows):
    hist = halo_ref[...]
    u_sc[0:HALO, :] = hist if first is False else jnp.where(first, jnp.zeros_like(hist), hist)
    u_sc[HALO:HALO + rows, :] = u
    halo_ref[...] = u[rows - HALO:rows, :]
    return (u * cw[2:3, :]
            + u_sc[HALO - 1:HALO - 1 + rows, :] * cw[1:2, :]
            + u_sc[HALO - 2:HALO - 2 + rows, :] * cw[0:1, :])


def _ffn_step(load_x, first, gpre_ref, gpost_ref, wup_ref, cw_ref, wdn_ref, o_ref,
              h_sc, u_sc, halo_sc, act_sc):
    rows = ROW_TILE
    c = FFN_CHUNK
    groups = o_ref.shape[0] // rows
    for r in range(groups):
        x = load_x(r)
        o_ref[r * rows:(r + 1) * rows, :] = x
        h_sc[r] = _rms(x, gpre_ref[...], NORM_EPS).astype(BF16)
    for j in range(D_FF // c):
        gs = slice(c * j, c * (j + 1))
        vs = slice(D_FF + c * j, D_FF + c * (j + 1))
        cw = jnp.concatenate([cw_ref[:, gs], cw_ref[:, vs]], axis=1)
        for r in range(groups):
            h = h_sc[r]
            u = jnp.concatenate(
                [jnp.dot(h, wup_ref[:, gs], preferred_element_type=F32),
                 jnp.dot(h, wup_ref[:, vs], preferred_element_type=F32)], axis=1)
            y = _causal_conv3(u_sc.at[r], halo_sc.at[j], u, cw, first if r == 0 else False, rows)
            gate = y[:, 0:c]
            act_sc[r, :, gs] = (gate * jax.nn.sigmoid(gate) * y[:, c:2 * c]).astype(BF16)
    for r in range(groups):
        m = jnp.dot(act_sc[r], wdn_ref[...], preferred_element_type=F32)
        rs = slice(r * rows, (r + 1) * rows)
        o_ref[rs, :] = o_ref[rs, :] + _rms(m, gpost_ref[...], NORM_EPS)


def _ffn_kernel(x_ref, gpre_ref, gpost_ref, wup_ref, cw_ref, wdn_ref, o_ref, *scratch):
    first = (pl.program_id(0) % (SEQ // x_ref.shape[0])) == 0

    def load_x(r):
        return x_ref[r * ROW_TILE:(r + 1) * ROW_TILE, :]

    _ffn_step(load_x, first, gpre_ref, gpost_ref, wup_ref, cw_ref, wdn_ref, o_ref, *scratch)


def _oproj_ffn_kernel(a_ref, wo_ref, gmix_ref, x_ref, gpre_ref, gpost_ref, wup_ref, cw_ref,
                      wdn_ref, o_ref, *scratch):
    first = (pl.program_id(0) % (SEQ // x_ref.shape[0])) == 0

    def load_x(r):
        rs = slice(r * ROW_TILE, (r + 1) * ROW_TILE)
        mix = jnp.dot(a_ref[rs, :], wo_ref[...], preferred_element_type=F32)
        return x_ref[rs, :] + _rms(mix, gmix_ref[...], NORM_EPS)

    _ffn_step(load_x, first, gpre_ref, gpost_ref, wup_ref, cw_ref, wdn_ref, o_ref, *scratch)


def _ffn(x2, g_pre, g_post, w_up, conv_w, w_down, layer, attn=None):
    n_rows, d = x2.shape
    c = FFN_CHUNK
    step_rows = FFN_GROUPS * ROW_TILE

    def layer_block(shape):
        return pl.BlockSpec((None,) + shape, lambda i: (layer, 0, 0),
                            pipeline_mode=pl.Buffered(1))

    row_spec = pl.BlockSpec((step_rows, d), lambda i: (i, 0))
    in_specs = [
        row_spec,
        _resident((1, d)),
        _resident((1, d)),
        layer_block((d, 2 * D_FF)),
        layer_block((CONV_WIDTH, 2 * D_FF)),
        layer_block((D_FF, d)),
    ]
    args = [x2, g_pre.reshape(1, d), g_post.reshape(1, d), w_up, conv_w, w_down]
    body = _ffn_kernel
    if attn is not None:
        a2, w_o, g_mix = attn
        in_specs = [pl.BlockSpec((step_rows, a2.shape[1]), lambda i: (i, 0)),
                    _resident(w_o.shape), _resident((1, d))] + in_specs
        args = [a2, w_o, g_mix.reshape(1, d)] + args
        body = _oproj_ffn_kernel
    return pl.pallas_call(
        body,
        grid=(n_rows // step_rows,),
        in_specs=in_specs,
        out_specs=row_spec,
        out_shape=jax.ShapeDtypeStruct((n_rows, d), F32),
        scratch_shapes=[
            pltpu.VMEM((FFN_GROUPS, ROW_TILE, d), BF16),
            pltpu.VMEM((FFN_GROUPS, HALO + ROW_TILE, 2 * c), F32),
            pltpu.VMEM((D_FF // c, HALO, 2 * c), F32),
            pltpu.VMEM((FFN_GROUPS, ROW_TILE, D_FF), BF16),
        ],
        compiler_params=pltpu.CompilerParams(
            dimension_semantics=("arbitrary",), vmem_limit_bytes=VMEM_LIMIT),
        name="conv_glu_ffn",
    )(*args)


def _mixer_kernel(x_ref, gpre_ref, gpost_ref, win_ref, cw_ref, wout_ref, o_ref,
                  h_sc, u_sc, halo_sc, y_sc):
    rows = x_ref.shape[0]
    c = MIX_CHUNK
    first = (pl.program_id(0) % (SEQ // rows)) == 0
    x = x_ref[...]
    h_sc[...] = _rms(x, gpre_ref[...], NORM_EPS).astype(BF16)
    for j in range(D_MODEL // c):
        cs = slice(c * j, c * (j + 1))
        h = h_sc[...]
        b_gate, c_gate, hv = (
            jnp.dot(h, win_ref[:, part * D_MODEL + c * j:part * D_MODEL + c * (j + 1)],
                    preferred_element_type=F32) for part in range(3))
        y = _causal_conv3(u_sc, halo_sc.at[j], c_gate * hv, cw_ref[:, cs], first, rows)
        y_sc[:, cs] = (b_gate * y).astype(BF16)
    m = jnp.dot(y_sc[...], wout_ref[...], preferred_element_type=F32)
    o_ref[...] = x + _rms(m, gpost_ref[...], NORM_EPS)


def _mixer(x2, g_pre, g_post, w_in, conv_w, w_out):
    n_rows, d = x2.shape
    c = MIX_CHUNK
    return pl.pallas_call(
        _mixer_kernel,
        grid=(n_rows // ROW_TILE,),
        in_specs=[
            pl.BlockSpec((ROW_TILE, d), lambda i: (i, 0)),
            _resident((1, d)),
            _resident((1, d)),
            _resident((d, 3 * d)),
            _resident((CONV_WIDTH, d)),
            _resident((d, d)),
        ],
        out_specs=pl.BlockSpec((ROW_TILE, d), lambda i: (i, 0)),
        out_shape=jax.ShapeDtypeStruct((n_rows, d), F32),
        scratch_shapes=[
            pltpu.VMEM((ROW_TILE, d), BF16),
            pltpu.VMEM((HALO + ROW_TILE, c), F32),
            pltpu.VMEM((d // c, HALO, c), F32),
            pltpu.VMEM((ROW_TILE, d), BF16),
        ],
        compiler_params=pltpu.CompilerParams(
            dimension_semantics=("arbitrary",), vmem_limit_bytes=VMEM_LIMIT),
        name="conv_mixer",
    )(x2, g_pre.reshape(1, d), g_post.reshape(1, d), w_in, conv_w, w_out)


def _qkv_layout(w):
    d = w.shape[0]
    qk = N_HEADS * QK_DIM
    parts = [w[:, i * qk:(i + 1) * qk].reshape(d, N_HEADS, QK_DIM) for i in range(4)]
    w_qk = jnp.concatenate(parts, axis=-1).reshape(d, QK_COLS).astype(BF16)
    w_vt = w[:, 4 * qk:].T.astype(BF16)
    return w_qk, w_vt


def kernel(x, norm_g, attn_w_qkv, attn_w_o, attn_lambda_q1, attn_lambda_k1, attn_lambda_q2,
           attn_lambda_k2, attn_subln_g, conv_w_in, conv_w, conv_w_out, ffn_w_up, ffn_conv_w,
           ffn_w_down):
    b, s, d = x.shape
    depth = norm_g.shape[0]
    slopes = jnp.exp2(-8.0 * jnp.arange(1, N_HEADS + 1, dtype=F32) / N_HEADS)
    x2 = x.reshape(b * s, d)
    for layer in range(depth):
        g = norm_g[layer]
        i = layer // 2
        if layer % 2 == 0:
            lambda_init = 0.8 - 0.6 * math.exp(-0.3 * layer)
            w_qk, w_vt = _qkv_layout(attn_w_qkv[i])
            qk, vt = _qkv_proj(x2.reshape(b, s, d), g[0], w_qk, w_vt)
            o = _attention(qk, vt, slopes, attn_lambda_q1[i], attn_lambda_k1[i],
                           attn_lambda_q2[i], attn_lambda_k2[i], attn_subln_g[i], lambda_init)
            attn = (o.reshape(b * s, d), attn_w_o[i].astype(BF16), g[1])
        else:
            x2 = _mixer(x2, g[0], g[1], conv_w_in[i].astype(BF16), conv_w[i],
                        conv_w_out[i].astype(BF16))
            attn = None
        x2 = _ffn(x2, g[2], g[3], ffn_w_up.astype(BF16), ffn_conv_w, ffn_w_down.astype(BF16),
                  layer, attn=attn)
    return x2.reshape(b, s, d)
```

```python
import functools
import math

import jax
import jax.numpy as jnp
from jax import lax
from jax.experimental import pallas as pl
from jax.experimental.pallas import tpu as pltpu

D_MODEL = 1024
SEQ = 2048
CHUNK = 64
N_HEADS = 8
QK_DIM = 64
V_DIM = 128
QK_COLS = N_HEADS * 4 * QK_DIM
CONV_WIDTH = 3
D_FF = 2816
NORM_EPS = 1e-6
SUBLN_EPS = 1e-5

F32 = jnp.float32
BF16 = jnp.bfloat16

ROW_TILE = 512
ATT_TILE = 256
SCORE_LEAD = 4
S_SLOTS = SCORE_LEAD + 1
ONES_ROWS = 16
FFN_GROUPS = 1
FFN_CHUNK = 256
MIX_CHUNK = 256
HALO = 8
VMEM_LIMIT = 56 * 1024 * 1024
MASK_VALUE = -1e30
LOG2E = math.log2(math.e)

_NT = (((1,), (1,)), ((), ()))


def _rms(x, g, eps):
    ms = jnp.mean(x * x, axis=-1, keepdims=True)
    return x * lax.rsqrt(ms + eps) * g


def _resident(shape):
    nd = len(shape)
    return pl.BlockSpec(shape, lambda *_: (0,) * nd, pipeline_mode=pl.Buffered(1))


def _qkv_proj_kernel(x_ref, g_ref, wqk_ref, scale_ref, wvt_ref, qk_ref, vt_ref, *, col_tile):
    h = _rms(x_ref[...], g_ref[...], NORM_EPS).astype(BF16)
    for j in range(QK_COLS // col_tile):
        cs = slice(j * col_tile, (j + 1) * col_tile)
        qk = jnp.dot(h, wqk_ref[:, cs], preferred_element_type=F32) * scale_ref[:, cs]
        qk_ref[:, cs] = qk.astype(BF16)
    vt_ref[...] = lax.dot_general(wvt_ref[...], h, _NT, preferred_element_type=F32).astype(BF16)


def _qkv_proj(x3, g, w_qk, w_vt):
    b, s, d = x3.shape
    tiles = s // ROW_TILE
    is_query = (jnp.arange(QK_COLS) % (4 * QK_DIM)) < 2 * QK_DIM
    col_scale = jnp.where(is_query, LOG2E * QK_DIM ** -0.5, 1.0).astype(F32).reshape(1, QK_COLS)
    return pl.pallas_call(
        functools.partial(_qkv_proj_kernel, col_tile=512),
        grid=(b, tiles),
        in_specs=[
            pl.BlockSpec((None, ROW_TILE, d), lambda bi, i: (bi, i, 0)),
            _resident((1, d)),
            _resident((d, QK_COLS)),
            _resident((1, QK_COLS)),
            _resident((N_HEADS * V_DIM, d)),
        ],
        out_specs=[
            pl.BlockSpec((None, ROW_TILE, QK_COLS), lambda bi, i: (bi, i, 0)),
            pl.BlockSpec((None, N_HEADS * V_DIM, ROW_TILE), lambda bi, i: (bi, 0, i)),
        ],
        out_shape=[
            jax.ShapeDtypeStruct((b, s, QK_COLS), BF16),
            jax.ShapeDtypeStruct((b, N_HEADS * V_DIM, s), BF16),
        ],
        compiler_params=pltpu.CompilerParams(
            dimension_semantics=("arbitrary", "arbitrary"), vmem_limit_bytes=VMEM_LIMIT),
        name="qkv_proj",
    )(x3, g.reshape(1, d), w_qk, col_scale, w_vt)


def _attn_kernel(slope_ref, lq1_ref, lk1_ref, lq2_ref, lk2_ref, g_ref, qk_ref, vt_ref, o_ref,
                 qs_sc, kb_sc, db_sc, acc_sc, s_sc, *, lambda_init):
    t = ATT_TILE
    nq = SEQ // t
    slope = slope_ref[pl.program_id(1)] * LOG2E

    lam =(jnp.exp(jnp.sum(lq1_ref[...] * lk1_ref[...], axis=-1, keepdims=True))
           - jnp.exp(jnp.sum(lq2_ref[...] * lk2_ref[...], axis=-1, keepdims=True))
           + lambda_init)

    lane = lax.broadcasted_iota(jnp.int32, (t, 2 * QK_DIM), 1)
    key_i = lax.broadcasted_iota(jnp.int32, (t, t), 0)
    qry_i = lax.broadcasted_iota(jnp.int32, (t, t), 1)
    allowed = (key_i // CHUNK) <= (qry_i // CHUNK)
    key_f = key_i.astype(F32)
    qry_f = qry_i.astype(F32)
    kb_sc[...] = slope * key_f
    db_sc[...] = jnp.where(allowed, slope * (qry_f - jnp.abs(qry_f - key_f)), MASK_VALUE)

    for qi in range(nq):
        qq = qk_ref[qi * t:(qi + 1) * t, 0:2 * QK_DIM]
        zero = jnp.zeros_like(qq)
        qs_sc[qi, 0:t, :] = jnp.where(lane < QK_DIM, qq, zero)
        qs_sc[qi, t:2 * t, :] = jnp.where(lane >= QK_DIM, qq, zero)

    tiles = [(kj, qi, mp) for kj in range(nq) for qi in range(kj, nq) for mp in range(2)]

    def emit_scores(idx):
        kj, qi, mp = tiles[idx]
        k_tile = qk_ref[kj * t:(kj + 1) * t, 2 * QK_DIM:4 * QK_DIM]
        bias = db_sc[...] if qi == kj else kb_sc[...]
        s = lax.dot_general(k_tile, qs_sc[qi, mp * t:(mp + 1) * t, :], _NT,
                            preferred_element_type=F32) + bias
        s_sc[idx % S_SLOTS] = s
        s_max[idx] = jnp.max(s, axis=0, keepdims=True)

    s_max = [None] * len(tiles)
    for idx in range(min(SCORE_LEAD, len(tiles))):
        emit_scores(idx)

    ones_rows = jnp.ones((ONES_ROWS, t), BF16)
    m = [[None, None] for _ in range(nq)]
    for idx, (kj, qi, mp) in enumerate(tiles):
        if idx + SCORE_LEAD < len(tiles):
            emit_scores(idx + SCORE_LEAD)
        s = s_sc[idx % S_SLOTS]
        v_ext = jnp.concatenate([vt_ref[:, kj * t:(kj + 1) * t], ones_rows], axis=0)
        shift = slope * jnp.full((1, t), float((qi - kj) * t), F32)
        m_tile = s_max[idx] - shift
        if kj == 0:
            m_new = m_tile
            p = jnp.exp2((s - (m_new + shift)).astype(BF16))
            acc_sc[qi, mp] = jnp.dot(v_ext, p, preferred_element_type=F32)
        else:
            m_new = jnp.maximum(m[qi][mp], m_tile)
            alpha = jnp.exp2(m[qi][mp] - m_new)
            p = jnp.exp2((s - (m_new + shift)).astype(BF16))
            acc_sc[qi, mp] = alpha * acc_sc[qi, mp] + jnp.dot(
                v_ext, p, preferred_element_type=F32)
        m[qi][mp] = m_new
        if qi == kj and mp == 1:
            acc1 = acc_sc[qi, 0]
            acc2 = acc_sc[qi, 1]
            o_t = (acc1[0:V_DIM] / acc1[V_DIM:V_DIM + 1]
                   - lam * (acc2[0:V_DIM] / acc2[V_DIM:V_DIM + 1]))
            ms = jnp.mean(o_t * o_t, axis=0, keepdims=True)
            o = (o_t * lax.rsqrt(ms + SUBLN_EPS)).T
            o = o * g_ref[...] * (1.0 - lambda_init)
            o_ref[qi * t:(qi + 1) * t, :] = o.astype(o_ref.dtype)


def _attention(qk, vt, slopes, lq1, lk1, lq2, lk2, subln_g, lambda_init):
    b = qk.shape[0]
    t = ATT_TILE
    smem = pl.BlockSpec(memory_space=pltpu.SMEM)
    lam_spec = _resident((1, QK_DIM))
    return pl.pallas_call(
        functools.partial(_attn_kernel, lambda_init=lambda_init),
        grid=(b, N_HEADS),
        in_specs=[
            smem, lam_spec, lam_spec, lam_spec, lam_spec, _resident((1, V_DIM)),
            pl.BlockSpec((None, SEQ, 4 * QK_DIM), lambda bi, h: (bi, 0, h)),
            pl.BlockSpec((None, V_DIM, SEQ), lambda bi, h: (bi, h, 0)),
        ],
        out_specs=pl.BlockSpec((None, SEQ, V_DIM), lambda bi, h: (bi, 0, h)),
        out_shape=jax.ShapeDtypeStruct((b, SEQ, N_HEADS * V_DIM), BF16),
        scratch_shapes=[
            pltpu.VMEM((SEQ // t, 2 * t, 2 * QK_DIM), BF16),
            pltpu.VMEM((t, t), F32),
            pltpu.VMEM((t, t), F32),
            pltpu.VMEM((SEQ // t, 2, V_DIM + ONES_ROWS, t), F32),
            pltpu.VMEM((S_SLOTS, t, t), F32),
        ],
        compiler_params=pltpu.CompilerParams(
            dimension_semantics=("arbitrary", "arbitrary"), vmem_limit_bytes=VMEM_LIMIT),
        name="diff_attention",
    )(slopes, lq1.reshape(1, QK_DIM), lk1.reshape(1, QK_DIM), lq2.reshape(1, QK_DIM),
      lk2.reshape(1, QK_DIM), subln_g.reshape(1, V_DIM), qk, vt)


def _causal_conv3(u_sc, halo_ref, u, cw, first, rows):
    hist = halo_ref[...]
    u_sc[0:HALO, :] = hist if first is False else jnp.where(first, jnp.zeros_like(hist), hist)
    u_sc[HALO:HALO + rows, :] = u
    halo_ref[...] = u[rows - HALO:rows, :]
    return (u * cw[2:3, :]
            + u_sc[HALO - 1:HALO - 1 + rows, :] * cw[1:2, :]
            + u_sc[HALO - 2:HALO - 2 + rows, :] * cw[0:1, :])


def _ffn_step(load_x, first, gpre_ref, gpost_ref, wup_ref, cw_ref, wdn_ref, o_ref,
              h_sc, u_sc, halo_sc, act_sc):
    rows = ROW_TILE
    c = FFN_CHUNK
    groups = o_ref.shape[0] // rows
    for r in range(groups):
        x = load_x(r)
        o_ref[r * rows:(r + 1) * rows, :] = x
        h_sc[r] = _rms(x, gpre_ref[...], NORM_EPS).astype(BF16)
    for j in range(D_FF // c):
        gs = slice(c * j, c * (j + 1))
        vs = slice(D_FF + c * j, D_FF + c * (j + 1))
        cw = jnp.concatenate([cw_ref[:, gs], cw_ref[:, vs]], axis=1)
        for r in range(groups):
            h = h_sc[r]
            u = jnp.concatenate(
                [jnp.dot(h, wup_ref[:, gs], preferred_element_type=F32),
                 jnp.dot(h, wup_ref[:, vs], preferred_element_type=F32)], axis=1)
            y = _causal_conv3(u_sc.at[r], halo_sc.at[j], u, cw, first if r == 0 else False, rows)
            gate = y[:, 0:c]
            act_sc[r, :, gs] = (gate * jax.nn.sigmoid(gate) * y[:, c:2 * c]).astype(BF16)
    for r in range(groups):
        m = jnp.dot(act_sc[r], wdn_ref[...], preferred_element_type=F32)
        rs = slice(r * rows, (r + 1) * rows)
        o_ref[rs, :] = o_ref[rs, :] + _rms(m, gpost_ref[...], NORM_EPS)


def _ffn_kernel(x_ref, gpre_ref, gpost_ref, wup_ref, cw_ref, wdn_ref, o_ref, *scratch):
    first = (pl.program_id(0) % (SEQ // x_ref.shape[0])) == 0

    def load_x(r):
        return x_ref[r * ROW_TILE:(r + 1) * ROW_TILE, :]

    _ffn_step(load_x, first, gpre_ref, gpost_ref, wup_ref, cw_ref, wdn_ref, o_ref, *scratch)


def _oproj_ffn_kernel(a_ref, wo_ref, gmix_ref, x_ref, gpre_ref, gpost_ref, wup_ref, cw_ref,
                      wdn_ref, o_ref, *scratch):
    first = (pl.program_id(0) % (SEQ // x_ref.shape[0])) == 0

    def load_x(r):
        rs = slice(r * ROW_TILE, (r + 1) * ROW_TILE)
        mix = jnp.dot(a_ref[rs, :], wo_ref[...], preferred_element_type=F32)
        return x_ref[rs, :] + _rms(mix, gmix_ref[...], NORM_EPS)

    _ffn_step(load_x, first, gpre_ref, gpost_ref, wup_ref, cw_ref, wdn_ref, o_ref, *scratch)


def _ffn(x2, g_pre, g_post, w_up, conv_w, w_down, layer, attn=None):
    n_rows, d = x2.shape
    c = FFN_CHUNK
    step_rows = FFN_GROUPS * ROW_TILE

    def layer_block(shape):
        return pl.BlockSpec((None,) + shape, lambda i: (layer, 0, 0),
                            pipeline_mode=pl.Buffered(1))

    row_spec = pl.BlockSpec((step_rows, d), lambda i: (i, 0))
    in_specs = [
        row_spec,
        _resident((1, d)),
        _resident((1, d)),
        layer_block((d, 2 * D_FF)),
        layer_block((CONV_WIDTH, 2 * D_FF)),
        layer_block((D_FF, d)),
    ]
    args = [x2, g_pre.reshape(1, d), g_post.reshape(1, d), w_up, conv_w, w_down]
    body = _ffn_kernel
    if attn is not None:
        a2, w_o, g_mix = attn
        in_specs = [pl.BlockSpec((step_rows, a2.shape[1]), lambda i: (i, 0)),
                    _resident(w_o.shape), _resident((1, d))] + in_specs
        args = [a2, w_o, g_mix.reshape(1, d)] + args
        body = _oproj_ffn_kernel
    return pl.pallas_call(
        body,
        grid=(n_rows // step_rows,),
        in_specs=in_specs,
        out_specs=row_spec,
        out_shape=jax.ShapeDtypeStruct((n_rows, d), F32),
        scratch_shapes=[
            pltpu.VMEM((FFN_GROUPS, ROW_TILE, d), BF16),
            pltpu.VMEM((FFN_GROUPS, HALO + ROW_TILE, 2 * c), F32),
            pltpu.VMEM((D_FF // c, HALO, 2 * c), F32),
            pltpu.VMEM((FFN_GROUPS, ROW_TILE, D_FF), BF16),
        ],
        compiler_params=pltpu.CompilerParams(
            dimension_semantics=("arbitrary",), vmem_limit_bytes=VMEM_LIMIT),
        name="conv_glu_ffn",
    )(*args)


def _mixer_kernel(x_ref, gpre_ref, gpost_ref, win_ref, cw_ref, wout_ref, o_ref,
                  h_sc, u_sc, halo_sc, y_sc):
    rows = x_ref.shape[0]
    c = MIX_CHUNK
    first = (pl.program_id(0) % (SEQ // rows)) == 0
    x = x_ref[...]
    h_sc[...] = _rms(x, gpre_ref[...], NORM_EPS).astype(BF16)
    for j in range(D_MODEL // c):
        cs = slice(c * j, c * (j + 1))
        h = h_sc[...]
        b_gate, c_gate, hv = (
            jnp.dot(h, win_ref[:, part * D_MODEL + c * j:part * D_MODEL + c * (j + 1)],
                    preferred_element_type=F32) for part in range(3))
        y = _causal_conv3(u_sc, halo_sc.at[j], c_gate * hv, cw_ref[:, cs], first, rows)
        y_sc[:, cs] = (b_gate * y).astype(BF16)
    m = jnp.dot(y_sc[...], wout_ref[...], preferred_element_type=F32)
    o_ref[...] = x + _rms(m, gpost_ref[...], NORM_EPS)


def _mixer(x2, g_pre, g_post, w_in, conv_w, w_out):
    n_rows, d = x2.shape
    c = MIX_CHUNK
    return pl.pallas_call(
        _mixer_kernel,
        grid=(n_rows // ROW_TILE,),
        in_specs=[
            pl.BlockSpec((ROW_TILE, d), lambda i: (i, 0)),
            _resident((1, d)),
            _resident((1, d)),
            _resident((d, 3 * d)),
            _resident((CONV_WIDTH, d)),
            _resident((d, d)),
        ],
        out_specs=pl.BlockSpec((ROW_TILE, d), lambda i: (i, 0)),
        out_shape=jax.ShapeDtypeStruct((n_rows, d), F32),
        scratch_shapes=[
            pltpu.VMEM((ROW_TILE, d), BF16),
            pltpu.VMEM((HALO + ROW_TILE, c), F32),
            pltpu.VMEM((d // c, HALO, c), F32),
            pltpu.VMEM((ROW_TILE, d), BF16),
        ],
        compiler_params=pltpu.CompilerParams(
            dimension_semantics=("arbitrary",), vmem_limit_bytes=VMEM_LIMIT),
        name="conv_mixer",
    )(x2, g_pre.reshape(1, d), g_post.reshape(1, d), w_in, conv_w, w_out)


def _qkv_layout(w):
    d = w.shape[0]
    qk = N_HEADS * QK_DIM
    parts = [w[:, i * qk:(i + 1) * qk].reshape(d, N_HEADS, QK_DIM) for i in range(4)]
    w_qk = jnp.concatenate(parts, axis=-1).reshape(d, QK_COLS).astype(BF16)
    w_vt = w[:, 4 * qk:].T.astype(BF16)
    return w_qk, w_vt


def kernel(x, norm_g, attn_w_qkv, attn_w_o, attn_lambda_q1, attn_lambda_k1, attn_lambda_q2,
           attn_lambda_k2, attn_subln_g, conv_w_in, conv_w, conv_w_out, ffn_w_up, ffn_conv_w,
           ffn_w_down):
    b, s, d = x.shape
    depth = norm_g.shape[0]
    slopes = jnp.exp2(-8.0 * jnp.arange(1, N_HEADS + 1, dtype=F32) / N_HEADS)
    x2 = x.reshape(b * s, d)
    for layer in range(depth):
        g = norm_g[layer]
        i = layer // 2
        if layer % 2 == 0:
            lambda_init = 0.8 - 0.6 * math.exp(-0.3 * layer)
            w_qk, w_vt = _qkv_layout(attn_w_qkv[i])
            qk, vt = _qkv_proj(x2.reshape(b, s, d), g[0], w_qk, w_vt)
            o = _attention(qk, vt, slopes, attn_lambda_q1[i], attn_lambda_k1[i],
                           attn_lambda_q2[i], attn_lambda_k2[i], attn_subln_g[i], lambda_init)
            attn = (o.reshape(b * s, d), attn_w_o[i].astype(BF16), g[1])
        else:
            x2 = _mixer(x2, g[0], g[1], conv_w_in[i].astype(BF16), conv_w[i],
                        conv_w_out[i].astype(BF16))
            attn = None
        x2 = _ffn(x2, g[2], g[3], ffn_w_up.astype(BF16), ffn_conv_w, ffn_w_down.astype(BF16),
                  layer, attn=attn)
    return x2.reshape(b, s, d)
```

```python
import functools
import math

import jax
import jax.numpy as jnp
from jax import lax
from jax.experimental import pallas as pl
from jax.experimental.pallas import tpu as pltpu

D_MODEL = 1024
SEQ = 2048
CHUNK = 64
N_HEADS = 8
QK_DIM = 64
V_DIM = 128
QK_COLS = N_HEADS * 4 * QK_DIM
CONV_WIDTH = 3
D_FF = 2816
NORM_EPS = 1e-6
SUBLN_EPS = 1e-5

F32 = jnp.float32
BF16 = jnp.bfloat16

ROW_TILE = 512
ATT_TILE = 256
SCORE_LEAD = 4
S_SLOTS = SCORE_LEAD + 1
BF16_SUBLANES = 16
ONES_ROWS = BF16_SUBLANES
FFN_CHUNK = 256
MIX_CHUNK = 256
HALO = 8
VMEM_LIMIT = 56 * 1024 * 1024
MASK_VALUE = -1e30
LOG2E = math.log2(math.e)

_NT = (((1,), (1,)), ((), ()))


def _rms(x, g, eps):
    ms = jnp.mean(x * x, axis=-1, keepdims=True)
    return x * lax.rsqrt(ms + eps) * g


def _resident(shape):
    nd = len(shape)
    return pl.BlockSpec(shape, lambda *_: (0,) * nd, pipeline_mode=pl.Buffered(1))


def _cast_job(w, layer, n_steps, step_of):
    rows, cols = w.shape[-2:]
    n_blocks = n_steps
    while rows % n_blocks or (rows // n_blocks) % BF16_SUBLANES:
        n_blocks //= 2
    every = n_steps // n_blocks
    rb = rows // n_blocks
    in_spec = pl.BlockSpec((None, rb, cols), lambda *g: (layer, step_of(*g) // every, 0))
    out_spec = pl.BlockSpec((rb, cols), lambda *g: (step_of(*g) // every, 0))
    return in_spec, out_spec, jax.ShapeDtypeStruct((rows, cols), BF16)


def _run_cast_jobs(srcs, dsts):
    for src, dst in zip(srcs, dsts):
        dst[...] = src[...].astype(BF16)


def _qkv_proj_kernel(x_ref, g_ref, wqk_ref, scale_ref, wvt_ref, *refs, col_tile, n_cast):
    cast_srcs, (qk_ref, vt_ref), cast_dsts = refs[:n_cast], refs[n_cast:n_cast + 2], refs[n_cast + 2:]
    h = _rms(x_ref[...], g_ref[...], NORM_EPS).astype(BF16)
    for j in range(QK_COLS // col_tile):
        cs = slice(j * col_tile, (j + 1) * col_tile)
        qk = jnp.dot(h, wqk_ref[:, cs], preferred_element_type=F32) * scale_ref[:, cs]
        qk_ref[:, cs] = qk.astype(BF16)
    vt_ref[...] = lax.dot_general(wvt_ref[...], h, _NT, preferred_element_type=F32).astype(BF16)
    _run_cast_jobs(cast_srcs, cast_dsts)


def _qkv_proj(x3, g, w_qk, w_vt, cast):
    b, s, d = x3.shape
    tiles = s // ROW_TILE
    is_query = (jnp.arange(QK_COLS) % (4 * QK_DIM)) < 2 * QK_DIM
    col_scale = jnp.where(is_query, LOG2E * QK_DIM ** -0.5, 1.0).astype(F32).reshape(1, QK_COLS)
    jobs = [_cast_job(w, layer, b * tiles, lambda bi, i: bi * tiles + i) for w, layer in cast]
    return pl.pallas_call(
        functools.partial(_qkv_proj_kernel, col_tile=512, n_cast=len(jobs)),
        grid=(b, tiles),
        in_specs=[
            pl.BlockSpec((None, ROW_TILE, d), lambda bi, i: (bi, i, 0)),
            _resident((1, d)),
            _resident((d, QK_COLS)),
            _resident((1, QK_COLS)),
            _resident((N_HEADS * V_DIM, d)),
        ] + [job[0] for job in jobs],
        out_specs=[
            pl.BlockSpec((None, ROW_TILE, QK_COLS), lambda bi, i: (bi, i, 0)),
            pl.BlockSpec((None, N_HEADS * V_DIM, ROW_TILE), lambda bi, i: (bi, 0, i)),
        ] + [job[1] for job in jobs],
        out_shape=[
            jax.ShapeDtypeStruct((b, s, QK_COLS), BF16),
            jax.ShapeDtypeStruct((b, N_HEADS * V_DIM, s), BF16),
        ] + [job[2] for job in jobs],
        compiler_params=pltpu.CompilerParams(
            dimension_semantics=("arbitrary", "arbitrary"), vmem_limit_bytes=VMEM_LIMIT),
        name="qkv_proj",
    )(x3, g.reshape(1, d), w_qk, col_scale, w_vt, *[w for w, _ in cast])


def _attn_kernel(slope_ref, lq1_ref, lk1_ref, lq2_ref, lk2_ref, g_ref, qk_ref, vt_ref, *refs,
                 lambda_init, n_cast):
    cast_srcs, o_ref, cast_dsts = refs[:n_cast], refs[n_cast], refs[n_cast + 1:2 * n_cast + 1]
    qs_sc, kb_sc, db_sc, acc_sc, s_sc = refs[2 * n_cast + 1:]
    t = ATT_TILE
    nq = SEQ // t
    slope = slope_ref[pl.program_id(1)] * LOG2E

    lam = (jnp.exp(jnp.sum(lq1_ref[...] * lk1_ref[...], axis=-1, keepdims=True))
           - jnp.exp(jnp.sum(lq2_ref[...] * lk2_ref[...], axis=-1, keepdims=True))
           + lambda_init)

    lane = lax.broadcasted_iota(jnp.int32, (t, 2 * QK_DIM), 1)
    key_i = lax.broadcasted_iota(jnp.int32, (t, t), 0)
    qry_i = lax.broadcasted_iota(jnp.int32, (t, t), 1)
    allowed = (key_i // CHUNK) <= (qry_i // CHUNK)
    key_f = key_i.astype(F32)
    qry_f = qry_i.astype(F32)
    kb_sc[...] = slope * key_f
    db_sc[...] = jnp.where(allowed, slope * (qry_f - jnp.abs(qry_f - key_f)), MASK_VALUE)

    for qi in range(nq):
        qq = qk_ref[qi * t:(qi + 1) * t, 0:2 * QK_DIM]
        zero = jnp.zeros_like(qq)
        qs_sc[qi, 0:t, :] = jnp.where(lane < QK_DIM, qq, zero)
        qs_sc[qi, t:2 * t, :] = jnp.where(lane >= QK_DIM, qq, zero)

    tiles = [(kj, qi, mp) for kj in range(nq) for qi in range(kj, nq) for mp in range(2)]

    def emit_scores(idx):
        kj, qi, mp = tiles[idx]
        k_tile = qk_ref[kj * t:(kj + 1) * t, 2 * QK_DIM:4 * QK_DIM]
        bias = db_sc[...] if qi == kj else kb_sc[...]
        s = lax.dot_general(k_tile, qs_sc[qi, mp * t:(mp + 1) * t, :], _NT,
                            preferred_element_type=F32) + bias
        s_sc[idx % S_SLOTS] = s
        s_max[idx] = jnp.max(s, axis=0, keepdims=True)

    s_max = [None] * len(tiles)
    for idx in range(min(SCORE_LEAD, len(tiles))):
        emit_scores(idx)

    ones_rows = jnp.ones((ONES_ROWS, t), BF16)
    m = [[None, None] for _ in range(nq)]
    for idx, (kj, qi, mp) in enumerate(tiles):
        if idx + SCORE_LEAD < len(tiles):
            emit_scores(idx + SCORE_LEAD)
        s = s_sc[idx % S_SLOTS]
        v_ext = jnp.concatenate([vt_ref[:, kj * t:(kj + 1) * t], ones_rows], axis=0)
        shift = slope * jnp.full((1, t), float((qi - kj) * t), F32)
        m_tile = s_max[idx] - shift
        if kj == 0:
            m_new = m_tile
            p = jnp.exp2((s - (m_new + shift)).astype(BF16))
            acc_sc[qi, mp] = jnp.dot(v_ext, p, preferred_element_type=F32)
        else:
            m_new = jnp.maximum(m[qi][mp], m_tile)
            alpha = jnp.exp2(m[qi][mp] - m_new)
            p = jnp.exp2((s - (m_new + shift)).astype(BF16))
            acc_sc[qi, mp] = alpha * acc_sc[qi, mp] + jnp.dot(
                v_ext, p, preferred_element_type=F32)
        m[qi][mp] = m_new
        if qi == kj and mp == 1:
            acc1 = acc_sc[qi, 0]
            acc2 = acc_sc[qi, 1]
            o_t = (acc1[0:V_DIM] / acc1[V_DIM:V_DIM + 1]
                   - lam * (acc2[0:V_DIM] / acc2[V_DIM:V_DIM + 1]))
            ms = jnp.mean(o_t * o_t, axis=0, keepdims=True)
            o = (o_t * lax.rsqrt(ms + SUBLN_EPS)).T
            o = o * g_ref[...] * (1.0 - lambda_init)
            o_ref[qi * t:(qi + 1) * t, :] = o.astype(o_ref.dtype)

    _run_cast_jobs(cast_srcs, cast_dsts)


def _attention(qk, vt, slopes, lq1, lk1, lq2, lk2, subln_g, lambda_init, cast):
    b = qk.shape[0]
    t = ATT_TILE
    smem = pl.BlockSpec(memory_space=pltpu.SMEM)
    lam_spec = _resident((1, QK_DIM))
    jobs = [_cast_job(w, layer, b * N_HEADS, lambda bi, h: bi * N_HEADS + h) for w, layer in cast]
    return pl.pallas_call(
        functools.partial(_attn_kernel, lambda_init=lambda_init, n_cast=len(jobs)),
        grid=(b, N_HEADS),
        in_specs=[
            smem, lam_spec, lam_spec, lam_spec, lam_spec, _resident((1, V_DIM)),
            pl.BlockSpec((None, SEQ, 4 * QK_DIM), lambda bi, h: (bi, 0, h)),
            pl.BlockSpec((None, V_DIM, SEQ), lambda bi, h: (bi, h, 0)),
        ] + [job[0] for job in jobs],
        out_specs=[pl.BlockSpec((None, SEQ, V_DIM), lambda bi, h: (bi, 0, h))]
        + [job[1] for job in jobs],
        out_shape=[jax.ShapeDtypeStruct((b, SEQ, N_HEADS * V_DIM), BF16)]
        + [job[2] for job in jobs],
        scratch_shapes=[
            pltpu.VMEM((SEQ // t, 2 * t, 2 * QK_DIM), BF16),
            pltpu.VMEM((t, t), F32),
            pltpu.VMEM((t, t), F32),
            pltpu.VMEM((SEQ // t, 2, V_DIM + ONES_ROWS, t), F32),
            pltpu.VMEM((S_SLOTS, t, t), F32),
        ],
        compiler_params=pltpu.CompilerParams(
            dimension_semantics=("arbitrary", "arbitrary"), vmem_limit_bytes=VMEM_LIMIT),
        name="diff_attention",
    )(slopes, lq1.reshape(1, QK_DIM), lk1.reshape(1, QK_DIM), lq2.reshape(1, QK_DIM),
      lk2.reshape(1, QK_DIM), subln_g.reshape(1, V_DIM), qk, vt, *[w for w, _ in cast])


def _causal_conv3(u_sc, halo_ref, u, cw, first, rows):
    hist = halo_ref[...]
    u_sc[0:HALO, :] = jnp.where(first, jnp.zeros_like(hist), hist)
    u_sc[HALO:HALO + rows, :] = u
    halo_ref[...] = u[rows - HALO:rows, :]
    return (u * cw[2:3, :]
            + u_sc[HALO - 1:HALO - 1 + rows, :] * cw[1:2, :]
            + u_sc[HALO - 2:HALO - 2 + rows, :] * cw[0:1, :])


def _ffn_tile(x, first, gpre_ref, gpost_ref, wup_ref, cw_ref, wdn_ref,
              h_sc, u_sc, halo_sc, act_sc):
    rows = x.shape[0]
    c = FFN_CHUNK
    h_sc[...] = _rms(x, gpre_ref[...], NORM_EPS).astype(BF16)
    for j in range(D_FF // c):
        gs = slice(c * j, c * (j + 1))
        vs = slice(D_FF + c * j, D_FF + c * (j + 1))
        h = h_sc[...]
        u = jnp.concatenate(
            [jnp.dot(h, wup_ref[:, gs], preferred_element_type=F32),
             jnp.dot(h, wup_ref[:, vs], preferred_element_type=F32)], axis=1)
        cw = jnp.concatenate([cw_ref[:, gs], cw_ref[:, vs]], axis=1)
        y = _causal_conv3(u_sc, halo_sc.at[j], u, cw, first, rows)
        gate = y[:, 0:c]
        act_sc[:, gs] = (gate * jax.nn.sigmoid(gate) * y[:, c:2 * c]).astype(BF16)
    m = jnp.dot(act_sc[...], wdn_ref[...], preferred_element_type=F32)
    return x + _rms(m, gpost_ref[...], NORM_EPS)


def _ffn_kernel(x_ref, gpre_ref, gpost_ref, wup_ref, cw_ref, wdn_ref, o_ref, *scratch):
    first = (pl.program_id(0) % (SEQ // x_ref.shape[0])) == 0
    o_ref[...] = _ffn_tile(x_ref[...], first, gpre_ref, gpost_ref, wup_ref, cw_ref, wdn_ref,
                           *scratch)


def _oproj_ffn_kernel(a_ref, wo_ref, gmix_ref, x_ref, gpre_ref, gpost_ref, wup_ref, cw_ref,
                      wdn_ref, o_ref, *scratch):
    first = (pl.program_id(0) % (SEQ // x_ref.shape[0])) == 0
    mix = jnp.dot(a_ref[...], wo_ref[...], preferred_element_type=F32)
    x = x_ref[...] + _rms(mix, gmix_ref[...], NORM_EPS)
    o_ref[...] = _ffn_tile(x, first, gpre_ref, gpost_ref, wup_ref, cw_ref, wdn_ref, *scratch)


def _ffn(x2, g_pre, g_post, w_up, conv_w, w_down, attn=None):
    n_rows, d = x2.shape
    c = FFN_CHUNK
    row_spec = pl.BlockSpec((ROW_TILE, d), lambda i: (i, 0))
    in_specs = [
        row_spec,
        _resident((1, d)),
        _resident((1, d)),
        _resident((d, 2 * D_FF)),
        _resident((CONV_WIDTH, 2 * D_FF)),
        _resident((D_FF, d)),
    ]
    args = [x2, g_pre.reshape(1, d), g_post.reshape(1, d), w_up, conv_w, w_down]
    body = _ffn_kernel
    if attn is not None:
        a2, w_o, g_mix = attn
        in_specs = [pl.BlockSpec((ROW_TILE, a2.shape[1]), lambda i: (i, 0)),
                    _resident(w_o.shape), _resident((1, d))] + in_specs
        args = [a2, w_o, g_mix.reshape(1, d)] + args
        body = _oproj_ffn_kernel
    return pl.pallas_call(
        body,
        grid=(n_rows // ROW_TILE,),
        in_specs=in_specs,
        out_specs=row_spec,
        out_shape=jax.ShapeDtypeStruct((n_rows, d), F32),
        scratch_shapes=[
            pltpu.VMEM((ROW_TILE, d), BF16),
            pltpu.VMEM((HALO + ROW_TILE, 2 * c), F32),
            pltpu.VMEM((D_FF // c, HALO, 2 * c), F32),
            pltpu.VMEM((ROW_TILE, D_FF), BF16),
        ],
        compiler_params=pltpu.CompilerParams(
            dimension_semantics=("arbitrary",), vmem_limit_bytes=VMEM_LIMIT),
        name="conv_glu_ffn",
    )(*args)


def _mixer_kernel(x_ref, gpre_ref, gpost_ref, win_ref, cw_ref, wout_ref, o_ref,
                  h_sc, u_sc, halo_sc, y_sc):
    rows = x_ref.shape[0]
    c = MIX_CHUNK
    first = (pl.program_id(0) % (SEQ // rows)) == 0
    x = x_ref[...]
    h_sc[...] = _rms(x, gpre_ref[...], NORM_EPS).astype(BF16)
    for j in range(D_MODEL // c):
        cs = slice(c * j, c * (j + 1))
        h = h_sc[...]
        b_gate, c_gate, hv = (
            jnp.dot(h, win_ref[:, part * D_MODEL + c * j:part * D_MODEL + c * (j + 1)],
                    preferred_element_type=F32) for part in range(3))
        y = _causal_conv3(u_sc, halo_sc.at[j], c_gate * hv, cw_ref[:, cs], first, rows)
        y_sc[:, cs] = (b_gate * y).astype(BF16)
    m = jnp.dot(y_sc[...], wout_ref[...], preferred_element_type=F32)
    o_ref[...] = x + _rms(m, gpost_ref[...], NORM_EPS)


def _mixer(x2, g_pre, g_post, w_in, conv_w, w_out):
    n_rows, d = x2.shape
    c = MIX_CHUNK
    return pl.pallas_call(
        _mixer_kernel,
        grid=(n_rows // ROW_TILE,),
        in_specs=[
            pl.BlockSpec((ROW_TILE, d), lambda i: (i, 0)),
            _resident((1, d)),
            _resident((1, d)),
            _resident((d, 3 * d)),
            _resident((CONV_WIDTH, d)),
            _resident((d, d)),
        ],
        out_specs=pl.BlockSpec((ROW_TILE, d), lambda i: (i, 0)),
        out_shape=jax.ShapeDtypeStruct((n_rows, d), F32),
        scratch_shapes=[
            pltpu.VMEM((ROW_TILE, d), BF16),
            pltpu.VMEM((HALO + ROW_TILE, c), F32),
            pltpu.VMEM((d // c, HALO, c), F32),
            pltpu.VMEM((ROW_TILE, d), BF16),
        ],
        compiler_params=pltpu.CompilerParams(
            dimension_semantics=("arbitrary",), vmem_limit_bytes=VMEM_LIMIT),
        name="conv_mixer",
    )(x2, g_pre.reshape(1, d), g_post.reshape(1, d), w_in, conv_w, w_out)


def _qkv_layout(w):
    d = w.shape[0]
    qk = N_HEADS * QK_DIM
    parts = [w[:, i * qk:(i + 1) * qk].reshape(d, N_HEADS, QK_DIM) for i in range(4)]
    w_qk = jnp.concatenate(parts, axis=-1).reshape(d, QK_COLS).astype(BF16)
    w_vt = w[:, 4 * qk:].T.astype(BF16)
    return w_qk, w_vt


def kernel(x, norm_g, attn_w_qkv, attn_w_o, attn_lambda_q1, attn_lambda_k1, attn_lambda_q2,
           attn_lambda_k2, attn_subln_g, conv_w_in, conv_w, conv_w_out, ffn_w_up, ffn_conv_w,
           ffn_w_down):
    b, s, d = x.shape
    assert (s, d) == (SEQ, D_MODEL) and norm_g.shape == (2, 4, d)
    assert attn_w_qkv.shape[0] == 1 and conv_w_in.shape[0] == 1 and ffn_w_up.shape[0] == 2
    slopes = jnp.exp2(-8.0 * jnp.arange(1, N_HEADS + 1, dtype=F32) / N_HEADS)
    x2 = x.reshape(b * s, d)

    g = norm_g[0]
    lambda_init = 0.8 - 0.6 * math.exp(-0.3 * 0)
    w_qk, w_vt = _qkv_layout(attn_w_qkv[0])
    qk, vt, w_o, w_up0, w_dn0 = _qkv_proj(
        x2.reshape(b, s, d), g[0], w_qk, w_vt,
        cast=[(attn_w_o, 0), (ffn_w_up, 0), (ffn_w_down, 0)])
    o, w_in, w_out, w_up1, w_dn1 = _attention(
        qk, vt, slopes, attn_lambda_q1[0], attn_lambda_k1[0], attn_lambda_q2[0],
        attn_lambda_k2[0], attn_subln_g[0], lambda_init,
        cast=[(conv_w_in, 0), (conv_w_out, 0), (ffn_w_up, 1), (ffn_w_down, 1)])
    x2 = _ffn(x2, g[2], g[3], w_up0, ffn_conv_w[0], w_dn0,
              attn=(o.reshape(b * s, d), w_o, g[1]))

    g = norm_g[1]
    x2 = _mixer(x2, g[0], g[1], w_in, conv_w[0], w_out)
    x2 = _ffn(x2, g[2], g[3], w_up1, ffn_conv_w[1], w_dn1)
    return x2.reshape(b, s, d)
```

```python
import functools
import math

import jax
import jax.numpy as jnp
from jax import lax
from jax.experimental import pallas as pl
from jax.experimental.pallas import tpu as pltpu

D_MODEL = 1024
SEQ = 2048
CHUNK = 64
N_HEADS = 8
QK_DIM = 64
V_DIM = 128
QK_COLS = N_HEADS * 4 * QK_DIM
CONV_WIDTH = 3
D_FF = 2816
NORM_EPS = 1e-6
SUBLN_EPS = 1e-5

F32 = jnp.float32
BF16 = jnp.bfloat16

ROW_TILE = 512
ATT_TILE = 256
ATT_HEADS = 2
SCORE_LEAD = 4
S_SLOTS = SCORE_LEAD + 1
BF16_SUBLANES = 16
ONES_ROWS = BF16_SUBLANES
FFN_CHUNK = 256
MIX_CHUNK = 256
HALO = 8
VMEM_LIMIT = 56 * 1024 * 1024
MASK_VALUE = -1e30
LOG2E = math.log2(math.e)

_NT = (((1,), (1,)), ((), ()))


def _rms(x, g, eps):
    ms = jnp.mean(x * x, axis=-1, keepdims=True)
    return x * lax.rsqrt(ms + eps) * g


def _resident(shape):
    nd = len(shape)
    return pl.BlockSpec(shape, lambda *_: (0,) * nd, pipeline_mode=pl.Buffered(1))


def _cast_job(w, layer, n_steps, step_of):
    rows, cols = w.shape[-2:]
    n_blocks = n_steps
    while rows % n_blocks or (rows // n_blocks) % BF16_SUBLANES:
        n_blocks //= 2
    every = n_steps // n_blocks
    rb = rows // n_blocks
    in_spec = pl.BlockSpec((None, rb, cols), lambda *g: (layer, step_of(*g) // every, 0))
    out_spec = pl.BlockSpec((rb, cols), lambda *g: (step_of(*g) // every, 0))
    return in_spec, out_spec, jax.ShapeDtypeStruct((rows, cols), BF16)


def _run_cast_jobs(srcs, dsts):
    for src, dst in zip(srcs, dsts):
        dst[...] = src[...].astype(BF16)


def _qkv_proj_kernel(x_ref, g_ref, wqk_ref, scale_ref, wvt_ref, *refs, col_tile, n_cast):
    cast_srcs, (qk_ref, vt_ref), cast_dsts = refs[:n_cast], refs[n_cast:n_cast + 2], refs[n_cast + 2:]
    h = _rms(x_ref[...], g_ref[...], NORM_EPS).astype(BF16)
    for j in range(QK_COLS // col_tile):
        cs = slice(j * col_tile, (j + 1) * col_tile)
        qk = jnp.dot(h, wqk_ref[:, cs], preferred_element_type=F32) * scale_ref[:, cs]
        qk_ref[:, cs] = qk.astype(BF16)
    vt_ref[...] = lax.dot_general(wvt_ref[...], h, _NT, preferred_element_type=F32).astype(BF16)
    _run_cast_jobs(cast_srcs, cast_dsts)


def _qkv_proj(x3, g, w_qk, w_vt, cast):
    b, s, d = x3.shape
    tiles = s // ROW_TILE
    is_query = (jnp.arange(QK_COLS) % (4 * QK_DIM)) < 2 * QK_DIM
    col_scale = jnp.where(is_query, LOG2E * QK_DIM ** -0.5, 1.0).astype(F32).reshape(1, QK_COLS)
    jobs = [_cast_job(w, layer, b * tiles, lambda bi, i: bi * tiles + i) for w, layer in cast]
    return pl.pallas_call(
        functools.partial(_qkv_proj_kernel, col_tile=512, n_cast=len(jobs)),
        grid=(b, tiles),
        in_specs=[
            pl.BlockSpec((None, ROW_TILE, d), lambda bi, i: (bi, i, 0)),
            _resident((1, d)),
            _resident((d, QK_COLS)),
            _resident((1, QK_COLS)),
            _resident((N_HEADS * V_DIM, d)),
        ] + [job[0] for job in jobs],
        out_specs=[
            pl.BlockSpec((None, ROW_TILE, QK_COLS), lambda bi, i: (bi, i, 0)),
            pl.BlockSpec((None, N_HEADS * V_DIM, ROW_TILE), lambda bi, i: (bi, 0, i)),
        ] + [job[1] for job in jobs],
        out_shape=[
            jax.ShapeDtypeStruct((b, s, QK_COLS), BF16),
            jax.ShapeDtypeStruct((b, N_HEADS * V_DIM, s), BF16),
        ] + [job[2] for job in jobs],
        compiler_params=pltpu.CompilerParams(
            dimension_semantics=("arbitrary", "arbitrary"), vmem_limit_bytes=VMEM_LIMIT),
        name="qkv_proj",
    )(x3, g.reshape(1, d), w_qk, col_scale, w_vt, *[w for w, _ in cast])


def _attn_kernel(slope_ref, lq1_ref, lk1_ref, lq2_ref, lk2_ref, g_ref, kb_ref, db_ref, qk_ref,
                 vt_ref, *refs, lambda_init, n_cast):
    cast_srcs, o_ref, cast_dsts = refs[:n_cast], refs[n_cast], refs[n_cast + 1:2 * n_cast + 1]
    scratch = refs[2 * n_cast + 1:]
    lam = (jnp.exp(jnp.sum(lq1_ref[...] * lk1_ref[...], axis=-1, keepdims=True))
           - jnp.exp(jnp.sum(lq2_ref[...] * lk2_ref[...], axis=-1, keepdims=True))
           + lambda_init)
    for hd in range(ATT_HEADS):
        _attn_head(slope_ref[pl.program_id(1) * ATT_HEADS + hd], lam, g_ref,
                   kb_ref.at[hd], db_ref.at[hd],
                   qk_ref.at[:, hd * 4 * QK_DIM:(hd + 1) * 4 * QK_DIM],
                   vt_ref.at[hd * V_DIM:(hd + 1) * V_DIM, :],
                   o_ref.at[:, hd * V_DIM:(hd + 1) * V_DIM],
                   *[sc.at[hd] for sc in scratch], lambda_init=lambda_init)
    _run_cast_jobs(cast_srcs, cast_dsts)


def _attn_head(slope, lam, g_ref, kb_sc, db_sc, qk_ref, vt_ref, o_ref, qs_sc, acc_sc, s_sc, *,
               lambda_init):
    t = ATT_TILE
    nq = SEQ // t
    slope = slope * LOG2E
    lane = lax.broadcasted_iota(jnp.int32, (t, 2 * QK_DIM), 1)

    for qi in range(nq):
        qq = qk_ref[qi * t:(qi + 1) * t, 0:2 * QK_DIM]
        zero = jnp.zeros_like(qq)
        qs_sc[qi, 0:t, :] = jnp.where(lane < QK_DIM, qq, zero)
        qs_sc[qi, t:2 * t, :] = jnp.where(lane >= QK_DIM, qq, zero)

    tiles = [(kj, qi, mp) for kj in range(nq) for qi in range(kj, nq) for mp in range(2)]

    def emit_scores(idx):
        kj, qi, mp = tiles[idx]
        k_tile = qk_ref[kj * t:(kj + 1) * t, 2 * QK_DIM:4 * QK_DIM]
        bias = db_sc[...] if qi == kj else kb_sc[...]
        s = lax.dot_general(k_tile, qs_sc[qi, mp * t:(mp + 1) * t, :], _NT,
                            preferred_element_type=F32) + bias
        s_sc[idx % S_SLOTS] = s
        s_max[idx] = jnp.max(s, axis=0, keepdims=True)

    s_max = [None] * len(tiles)
    for idx in range(min(SCORE_LEAD, len(tiles))):
        emit_scores(idx)

    ones_rows = jnp.ones((ONES_ROWS, t), BF16)
    m = [[None, None] for _ in range(nq)]
    for idx, (kj, qi, mp) in enumerate(tiles):
        if idx + SCORE_LEAD < len(tiles):
            emit_scores(idx + SCORE_LEAD)
        s = s_sc[idx % S_SLOTS]
        v_ext = jnp.concatenate([vt_ref[:, kj * t:(kj + 1) * t], ones_rows], axis=0)
        shift = slope * jnp.full((1, t), float((qi - kj) * t), F32)
        m_tile = s_max[idx] - shift
        if kj == 0:
            m_new = m_tile
            p = jnp.exp2((s - (m_new + shift)).astype(BF16))
            acc_sc[qi, mp] = jnp.dot(v_ext, p, preferred_element_type=F32)
        else:
            m_new = jnp.maximum(m[qi][mp], m_tile)
            alpha = jnp.exp2(m[qi][mp] - m_new)
            p = jnp.exp2((s - (m_new + shift)).astype(BF16))
            acc_sc[qi, mp] = alpha * acc_sc[qi, mp] + jnp.dot(
                v_ext, p, preferred_element_type=F32)
        m[qi][mp] = m_new
        if qi == kj and mp == 1:
            acc1 = acc_sc[qi, 0]
            acc2 = acc_sc[qi, 1]
            o_t = (acc1[0:V_DIM] / acc1[V_DIM:V_DIM + 1]
                   - lam * (acc2[0:V_DIM] / acc2[V_DIM:V_DIM + 1]))
            ms = jnp.mean(o_t * o_t, axis=0, keepdims=True)
            o = (o_t * lax.rsqrt(ms + SUBLN_EPS)).T
            o = o * g_ref[...] * (1.0 - lambda_init)
            o_ref[qi * t:(qi + 1) * t, :] = o.astype(o_ref.dtype)


def _attention(qk, vt, slopes, lq1, lk1, lq2, lk2, subln_g, lambda_init, cast):
    b = qk.shape[0]
    t = ATT_TILE
    smem = pl.BlockSpec(memory_space=pltpu.SMEM)
    lam_spec = _resident((1, QK_DIM))
    hd = ATT_HEADS
    steps = N_HEADS // hd
    jobs = [_cast_job(w, layer, b * steps, lambda bi, h: bi * steps + h) for w, layer in cast]
    key = jnp.arange(t, dtype=F32)[:, None]
    qry = jnp.arange(t, dtype=F32)[None, :]
    slope2 = (slopes * LOG2E)[:, None, None]
    kb = slope2 * jnp.broadcast_to(key, (t, t))
    db = jnp.where((key // CHUNK) <= (qry // CHUNK), slope2 * (qry - jnp.abs(qry - key)),
                   MASK_VALUE)
    table_spec = pl.BlockSpec((hd, t, t), lambda bi, h: (h, 0, 0))
    return pl.pallas_call(
        functools.partial(_attn_kernel, lambda_init=lambda_init, n_cast=len(jobs)),
        grid=(b, steps),
        in_specs=[
            smem, lam_spec, lam_spec, lam_spec, lam_spec, _resident((1, V_DIM)),
            table_spec, table_spec,
            pl.BlockSpec((None, SEQ, hd * 4 * QK_DIM), lambda bi, h: (bi, 0, h)),
            pl.BlockSpec((None, hd * V_DIM, SEQ), lambda bi, h: (bi, h, 0)),
        ] + [job[0] for job in jobs],
        out_specs=[pl.BlockSpec((None, SEQ, hd * V_DIM), lambda bi, h: (bi, 0, h))]
        + [job[1] for job in jobs],
        out_shape=[jax.ShapeDtypeStruct((b, SEQ, N_HEADS * V_DIM), BF16)]
        + [job[2] for job in jobs],
        scratch_shapes=[
            pltpu.VMEM((hd, SEQ // t, 2 * t, 2 * QK_DIM), BF16),
            pltpu.VMEM((hd, SEQ // t, 2, V_DIM + ONES_ROWS, t), F32),
            pltpu.VMEM((hd, S_SLOTS, t, t), F32),
        ],
        compiler_params=pltpu.CompilerParams(
            dimension_semantics=("arbitrary", "arbitrary"), vmem_limit_bytes=VMEM_LIMIT),
        name="diff_attention",
    )(slopes, lq1.reshape(1, QK_DIM), lk1.reshape(1, QK_DIM), lq2.reshape(1, QK_DIM),
      lk2.reshape(1, QK_DIM), subln_g.reshape(1, V_DIM), kb, db, qk, vt,
      *[w for w, _ in cast])


def _causal_conv3(u_sc, halo_ref, u, cw, first, rows):
    hist = halo_ref[...]
    u_sc[0:HALO, :] = jnp.where(first, jnp.zeros_like(hist), hist)
    u_sc[HALO:HALO + rows, :] = u
    halo_ref[...] = u[rows - HALO:rows, :]
    return (u * cw[2:3, :]
            + u_sc[HALO - 1:HALO - 1 + rows, :] * cw[1:2, :]
            + u_sc[HALO - 2:HALO - 2 + rows, :] * cw[0:1, :])


def _ffn_tile(x, first, gpre_ref, gpost_ref, wup_ref, cw_ref, wdn_ref,
              h_sc, u_sc, halo_sc, act_sc):
    rows = x.shape[0]
    c = FFN_CHUNK
    h_sc[...] = _rms(x, gpre_ref[...], NORM_EPS).astype(BF16)
    for j in range(D_FF // c):
        gs = slice(c * j, c * (j + 1))
        vs = slice(D_FF + c * j, D_FF + c * (j + 1))
        h = h_sc[...]
        u = jnp.concatenate(
            [jnp.dot(h, wup_ref[:, gs], preferred_element_type=F32),
             jnp.dot(h, wup_ref[:, vs], preferred_element_type=F32)], axis=1)
        cw = jnp.concatenate([cw_ref[:, gs], cw_ref[:, vs]], axis=1)
        y = _causal_conv3(u_sc, halo_sc.at[j], u, cw, first, rows)
        gate = y[:, 0:c]
        act_sc[:, gs] = (gate * jax.nn.sigmoid(gate) * y[:, c:2 * c]).astype(BF16)
    m = jnp.dot(act_sc[...], wdn_ref[...], preferred_element_type=F32)
    return x + _rms(m, gpost_ref[...], NORM_EPS)


def _ffn_kernel(x_ref, gpre_ref, gpost_ref, wup_ref, cw_ref, wdn_ref, o_ref, *scratch):
    first = (pl.program_id(0) % (SEQ // x_ref.shape[0])) == 0
    o_ref[...] = _ffn_tile(x_ref[...], first, gpre_ref, gpost_ref, wup_ref, cw_ref, wdn_ref,
                           *scratch)


def _oproj_ffn_kernel(a_ref, wo_ref, gmix_ref, x_ref, gpre_ref, gpost_ref, wup_ref, cw_ref,
                      wdn_ref, o_ref, *scratch):
    first = (pl.program_id(0) % (SEQ // x_ref.shape[0])) == 0
    mix = jnp.dot(a_ref[...], wo_ref[...], preferred_element_type=F32)
    x = x_ref[...] + _rms(mix, gmix_ref[...], NORM_EPS)
    o_ref[...] = _ffn_tile(x, first, gpre_ref, gpost_ref, wup_ref, cw_ref, wdn_ref, *scratch)


def _ffn(x2, g_pre, g_post, w_up, conv_w, w_down, attn=None):
    n_rows, d = x2.shape
    c = FFN_CHUNK
    row_spec = pl.BlockSpec((ROW_TILE, d), lambda i: (i, 0))
    in_specs = [
        row_spec,
        _resident((1, d)),
        _resident((1, d)),
        _resident((d, 2 * D_FF)),
        _resident((CONV_WIDTH, 2 * D_FF)),
        _resident((D_FF, d)),
    ]
    args = [x2, g_pre.reshape(1, d), g_post.reshape(1, d), w_up, conv_w, w_down]
    body = _ffn_kernel
    if attn is not None:
        a2, w_o, g_mix = attn
        in_specs = [pl.BlockSpec((ROW_TILE, a2.shape[1]), lambda i: (i, 0)),
                    _resident(w_o.shape), _resident((1, d))] + in_specs
        args = [a2, w_o, g_mix.reshape(1, d)] + args
        body = _oproj_ffn_kernel
    return pl.pallas_call(
        body,
        grid=(n_rows // ROW_TILE,),
        in_specs=in_specs,
        out_specs=row_spec,
        out_shape=jax.ShapeDtypeStruct((n_rows, d), F32),
        scratch_shapes=[
            pltpu.VMEM((ROW_TILE, d), BF16),
            pltpu.VMEM((HALO + ROW_TILE, 2 * c), F32),
            pltpu.VMEM((D_FF // c, HALO, 2 * c), F32),
            pltpu.VMEM((ROW_TILE, D_FF), BF16),
        ],
        compiler_params=pltpu.CompilerParams(
            dimension_semantics=("arbitrary",), vmem_limit_bytes=VMEM_LIMIT),
        name="conv_glu_ffn",
    )(*args)


def _mixer_kernel(x_ref, gpre_ref, gpost_ref, win_ref, cw_ref, wout_ref, o_ref,
                  h_sc, u_sc, halo_sc, y_sc):
    rows = x_ref.shape[0]
    c = MIX_CHUNK
    first = (pl.program_id(0) % (SEQ // rows)) == 0
    x = x_ref[...]
    h_sc[...] = _rms(x, gpre_ref[...], NORM_EPS).astype(BF16)
    for j in range(D_MODEL // c):
        cs = slice(c * j, c * (j + 1))
        h = h_sc[...]
        b_gate, c_gate, hv = (
            jnp.dot(h, win_ref[:, part * D_MODEL + c * j:part * D_MODEL + c * (j + 1)],
                    preferred_element_type=F32) for part in range(3))
        y = _causal_conv3(u_sc, halo_sc.at[j], c_gate * hv, cw_ref[:, cs], first, rows)
        y_sc[:, cs] = (b_gate * y).astype(BF16)
    m = jnp.dot(y_sc[...], wout_ref[...], preferred_element_type=F32)
    o_ref[...] = x + _rms(m, gpost_ref[...], NORM_EPS)


def _mixer(x2, g_pre, g_post, w_in, conv_w, w_out):
    n_rows, d = x2.shape
    c = MIX_CHUNK
    return pl.pallas_call(
        _mixer_kernel,
        grid=(n_rows // ROW_TILE,),
        in_specs=[
            pl.BlockSpec((ROW_TILE, d), lambda i: (i, 0)),
            _resident((1, d)),
            _resident((1, d)),
            _resident((d, 3 * d)),
            _resident((CONV_WIDTH, d)),
            _resident((d, d)),
        ],
        out_specs=pl.BlockSpec((ROW_TILE, d), lambda i: (i, 0)),
        out_shape=jax.ShapeDtypeStruct((n_rows, d), F32),
        scratch_shapes=[
            pltpu.VMEM((ROW_TILE, d), BF16),
            pltpu.VMEM((HALO + ROW_TILE, c), F32),
            pltpu.VMEM((d // c, HALO, c), F32),
            pltpu.VMEM((ROW_TILE, d), BF16),
        ],
        compiler_params=pltpu.CompilerParams(
            dimension_semantics=("arbitrary",), vmem_limit_bytes=VMEM_LIMIT),
        name="conv_mixer",
    )(x2, g_pre.reshape(1, d), g_post.reshape(1, d), w_in, conv_w, w_out)


def _qkv_layout(w):
    d = w.shape[0]
    qk = N_HEADS * QK_DIM
    parts = [w[:, i * qk:(i + 1) * qk].reshape(d, N_HEADS, QK_DIM) for i in range(4)]
    w_qk = jnp.concatenate(parts, axis=-1).reshape(d, QK_COLS).astype(BF16)
    w_vt = w[:, 4 * qk:].T.astype(BF16)
    return w_qk, w_vt


def kernel(x, norm_g, attn_w_qkv, attn_w_o, attn_lambda_q1, attn_lambda_k1, attn_lambda_q2,
           attn_lambda_k2, attn_subln_g, conv_w_in, conv_w, conv_w_out, ffn_w_up, ffn_conv_w,
           ffn_w_down):
    b, s, d = x.shape
    assert (s, d) == (SEQ, D_MODEL) and norm_g.shape == (2, 4, d)
    assert attn_w_qkv.shape[0] == 1 and conv_w_in.shape[0] == 1 and ffn_w_up.shape[0] == 2
    slopes = jnp.exp2(-8.0 * jnp.arange(1, N_HEADS + 1, dtype=F32) / N_HEADS)
    x2 = x.reshape(b * s, d)

    g = norm_g[0]
    lambda_init = 0.8 - 0.6 * math.exp(-0.3 * 0)
    w_qk, w_vt = _qkv_layout(attn_w_qkv[0])
    qk, vt, w_o, w_up0, w_dn0 = _qkv_proj(
        x2.reshape(b, s, d), g[0], w_qk, w_vt,
        cast=[(attn_w_o, 0), (ffn_w_up, 0), (ffn_w_down, 0)])
    o, w_in, w_out, w_up1, w_dn1 = _attention(
        qk, vt, slopes, attn_lambda_q1[0], attn_lambda_k1[0], attn_lambda_q2[0],
        attn_lambda_k2[0], attn_subln_g[0], lambda_init,
        cast=[(conv_w_in, 0), (conv_w_out, 0), (ffn_w_up, 1), (ffn_w_down, 1)])
    x2 = _ffn(x2, g[2], g[3], w_up0, ffn_conv_w[0], w_dn0,
              attn=(o.reshape(b * s, d), w_o, g[1]))

    g = norm_g[1]
    x2 = _mixer(x2, g[0], g[1], w_in, conv_w[0], w_out)
    x2 = _ffn(x2, g[2], g[3], w_up1, ffn_conv_w[1], w_dn1)
    return x2.reshape(b, s, d)
```

```python
import functools
import math

import jax
import jax.numpy as jnp
from jax import lax
from jax.experimental import pallas as pl
from jax.experimental.pallas import tpu as pltpu

D_MODEL = 1024
SEQ = 2048
CHUNK = 64
N_HEADS = 8
QK_DIM = 64
V_DIM = 128
QK_COLS = N_HEADS * 4 * QK_DIM
CONV_WIDTH = 3
D_FF = 2816
NORM_EPS = 1e-6
SUBLN_EPS = 1e-5

F32 = jnp.float32
BF16 = jnp.bfloat16

ROW_TILE = 512
ATT_TILE = 256
ATT_HEADS = 2
SCORE_LEAD = 4
S_SLOTS = SCORE_LEAD + 1
BF16_SUBLANES = 16
ONES_ROWS = BF16_SUBLANES
FFN_CHUNK = 256
MIX_CHUNK = 256
HALO = 8
VMEM_LIMIT = 56 * 1024 * 1024
MASK_VALUE = -1e30
LOG2E = math.log2(math.e)

_NT = (((1,), (1,)), ((), ()))


def _rms(x, g, eps):
    ms = jnp.mean(x * x, axis=-1, keepdims=True)
    return x * lax.rsqrt(ms + eps) * g


def _resident(shape):
    nd = len(shape)
    return pl.BlockSpec(shape, lambda *_: (0,) * nd, pipeline_mode=pl.Buffered(1))


def _cast_job(w, layer, n_steps, step_of):
    rows, cols = w.shape[-2:]
    n_blocks = n_steps
    while rows % n_blocks or (rows // n_blocks) % BF16_SUBLANES:
        n_blocks //= 2
    every = n_steps // n_blocks
    rb = rows // n_blocks
    in_spec = pl.BlockSpec((None, rb, cols), lambda *g: (layer, step_of(*g) // every, 0))
    out_spec = pl.BlockSpec((rb, cols), lambda *g: (step_of(*g) // every, 0))
    return in_spec, out_spec, jax.ShapeDtypeStruct((rows, cols), BF16)


def _run_cast_jobs(srcs, dsts):
    for src, dst in zip(srcs, dsts):
        dst[...] = src[...].astype(BF16)


def _qkv_proj_kernel(x_ref, g_ref, wk_ref, wqvt_ref, *refs, n_cast):
    cast_srcs, (k_ref, qvt_ref), cast_dsts = refs[:n_cast], refs[n_cast:n_cast + 2], refs[n_cast + 2:]
    h = _rms(x_ref[...], g_ref[...], NORM_EPS).astype(BF16)
    k_ref[...] = jnp.dot(h, wk_ref[...], preferred_element_type=F32).astype(BF16)
    qvt = lax.dot_general(wqvt_ref[...], h, _NT, preferred_element_type=F32)
    n_q = N_HEADS * 2 * QK_DIM
    qvt_ref[0:n_q, :] = (qvt[0:n_q] * (LOG2E * QK_DIM ** -0.5)).astype(BF16)
    qvt_ref[n_q:, :] = qvt[n_q:].astype(BF16)
    _run_cast_jobs(cast_srcs, cast_dsts)


def _qkv_proj(x3, g, w_k, w_qvt, cast):
    b, s, d = x3.shape
    tiles = s // ROW_TILE
    n_k, n_qv = w_k.shape[1], w_qvt.shape[0]
    jobs = [_cast_job(w, layer, b * tiles, lambda bi, i: bi * tiles + i) for w, layer in cast]
    return pl.pallas_call(
        functools.partial(_qkv_proj_kernel, n_cast=len(jobs)),
        grid=(b, tiles),
        in_specs=[
            pl.BlockSpec((None, ROW_TILE, d), lambda bi, i: (bi, i, 0)),
            _resident((1, d)),
            _resident((d, n_k)),
            _resident((n_qv, d)),
        ] + [job[0] for job in jobs],
        out_specs=[
            pl.BlockSpec((None, ROW_TILE, n_k), lambda bi, i: (bi, i, 0)),
            pl.BlockSpec((None, n_qv, ROW_TILE), lambda bi, i: (bi, 0, i)),
        ] + [job[1] for job in jobs],
        out_shape=[
            jax.ShapeDtypeStruct((b, s, n_k), BF16),
            jax.ShapeDtypeStruct((b, n_qv, s), BF16),
        ] + [job[2] for job in jobs],
        compiler_params=pltpu.CompilerParams(
            dimension_semantics=("arbitrary", "arbitrary"), vmem_limit_bytes=VMEM_LIMIT),
        name="qkv_proj",
    )(x3, g.reshape(1, d), w_k, w_qvt, *[w for w, _ in cast])


def _attn_kernel(slope_ref, lq1_ref, lk1_ref, lq2_ref, lk2_ref, g_ref, kb_ref, db_ref, k_ref,
                 qt_ref, vt_ref, *refs, lambda_init, n_cast):
    cast_srcs, o_ref, cast_dsts = refs[:n_cast], refs[n_cast], refs[n_cast + 1:2 * n_cast + 1]
    scratch = refs[2 * n_cast + 1:]
    lam = (jnp.exp(jnp.sum(lq1_ref[...] * lk1_ref[...], axis=-1, keepdims=True))
           - jnp.exp(jnp.sum(lq2_ref[...] * lk2_ref[...], axis=-1, keepdims=True))
           + lambda_init)
    for hd in range(ATT_HEADS):
        _attn_head(slope_ref[pl.program_id(1) * ATT_HEADS + hd], lam, g_ref,
                   kb_ref.at[hd], db_ref.at[hd],
                   k_ref.at[:, hd * 2 * QK_DIM:(hd + 1) * 2 * QK_DIM],
                   qt_ref.at[hd * 2 * QK_DIM:(hd + 1) * 2 * QK_DIM, :],
                   vt_ref.at[hd * V_DIM:(hd + 1) * V_DIM, :],
                   o_ref.at[:, hd * V_DIM:(hd + 1) * V_DIM],
                   *[sc.at[hd] for sc in scratch], lambda_init=lambda_init)
    _run_cast_jobs(cast_srcs, cast_dsts)


def _attn_head(slope, lam, g_ref, kb_sc, db_sc, k_ref, qt_ref, vt_ref, o_ref, qs_sc, acc_sc,
               s_sc, *, lambda_init):
    t = ATT_TILE
    nq = SEQ // t
    slope = slope * LOG2E
    row = lax.broadcasted_iota(jnp.int32, (2 * QK_DIM, t), 0)

    for qi in range(nq):
        qt = qt_ref[:, qi * t:(qi + 1) * t]
        zero = jnp.zeros_like(qt)
        qs_sc[qi, 0] = jnp.where(row < QK_DIM, qt, zero)
        qs_sc[qi, 1] = jnp.where(row >= QK_DIM, qt, zero)

    tiles = [(kj, qi, mp) for kj in range(nq) for qi in range(kj, nq) for mp in range(2)]

    def emit_scores(idx):
        kj, qi, mp = tiles[idx]
        k_tile = k_ref[kj * t:(kj + 1) * t, :]
        bias = db_sc[...] if qi == kj else kb_sc[...]
        s = jnp.dot(k_tile, qs_sc[qi, mp], preferred_element_type=F32) + bias
        s_sc[idx % S_SLOTS] = s
        s_max[idx] = jnp.max(s, axis=0, keepdims=True)

    s_max = [None] * len(tiles)
    for idx in range(min(SCORE_LEAD, len(tiles))):
        emit_scores(idx)

    ones_rows = jnp.ones((ONES_ROWS, t), BF16)
    m = [[None, None] for _ in range(nq)]
    for idx, (kj, qi, mp) in enumerate(tiles):
        if idx + SCORE_LEAD < len(tiles):
            emit_scores(idx + SCORE_LEAD)
        s = s_sc[idx % S_SLOTS]
        v_ext = jnp.concatenate([vt_ref[:, kj * t:(kj + 1) * t], ones_rows], axis=0)
        shift = slope * jnp.full((1, t), float((qi - kj) * t), F32)
        m_tile = s_max[idx] - shift
        if kj == 0:
            m_new = m_tile
            p = jnp.exp2((s - (m_new + shift)).astype(BF16))
            acc_sc[qi, mp] = jnp.dot(v_ext, p, preferred_element_type=F32)
        else:
            m_new = jnp.maximum(m[qi][mp], m_tile)
            alpha = jnp.exp2(m[qi][mp] - m_new)
            p = jnp.exp2((s - (m_new + shift)).astype(BF16))
            acc_sc[qi, mp] = alpha * acc_sc[qi, mp] + jnp.dot(
                v_ext, p, preferred_element_type=F32)
        m[qi][mp] = m_new
        if qi == kj and mp == 1:
            acc1 = acc_sc[qi, 0]
            acc2 = acc_sc[qi, 1]
            o_t = (acc1[0:V_DIM] / acc1[V_DIM:V_DIM + 1]
                   - lam * (acc2[0:V_DIM] / acc2[V_DIM:V_DIM + 1]))
            ms = jnp.mean(o_t * o_t, axis=0, keepdims=True)
            o = (o_t * lax.rsqrt(ms + SUBLN_EPS)).T
            o = o * g_ref[...] * (1.0 - lambda_init)
            o_ref[qi * t:(qi + 1) * t, :] = o.astype(o_ref.dtype)


def _attention(k, qvt, slopes, lq1, lk1, lq2, lk2, subln_g, lambda_init, cast):
    b = k.shape[0]
    t = ATT_TILE
    smem = pl.BlockSpec(memory_space=pltpu.SMEM)
    lam_spec = _resident((1, QK_DIM))
    hd = ATT_HEADS
    steps = N_HEADS // hd
    jobs = [_cast_job(w, layer, b * steps, lambda bi, h: bi * steps + h) for w, layer in cast]
    key = jnp.arange(t, dtype=F32)[:, None]
    qry = jnp.arange(t, dtype=F32)[None, :]
    slope2 = (slopes * LOG2E)[:, None, None]
    kb = slope2 * jnp.broadcast_to(key, (t, t))
    db = jnp.where((key // CHUNK) <= (qry // CHUNK), slope2 * (qry - jnp.abs(qry - key)),
                   MASK_VALUE)
    table_spec = pl.BlockSpec((hd, t, t), lambda bi, h: (h, 0, 0))
    return pl.pallas_call(
        functools.partial(_attn_kernel, lambda_init=lambda_init, n_cast=len(jobs)),
        grid=(b, steps),
        in_specs=[
            smem, lam_spec, lam_spec, lam_spec, lam_spec, _resident((1, V_DIM)),
            table_spec, table_spec,
            pl.BlockSpec((None, SEQ, hd * 2 * QK_DIM), lambda bi, h: (bi, 0, h)),
            pl.BlockSpec((None, hd * 2 * QK_DIM, SEQ), lambda bi, h: (bi, h, 0)),
            pl.BlockSpec((None, hd * V_DIM, SEQ), lambda bi, h: (bi, steps + h, 0)),
        ] + [job[0] for job in jobs],
        out_specs=[pl.BlockSpec((None, SEQ, hd * V_DIM), lambda bi, h: (bi, 0, h))]
        + [job[1] for job in jobs],
        out_shape=[jax.ShapeDtypeStruct((b, SEQ, N_HEADS * V_DIM), BF16)]
        + [job[2] for job in jobs],
        scratch_shapes=[
            pltpu.VMEM((hd, SEQ // t, 2, 2 * QK_DIM, t), BF16),
            pltpu.VMEM((hd, SEQ // t, 2, V_DIM + ONES_ROWS, t), F32),
            pltpu.VMEM((hd, S_SLOTS, t, t), F32),
        ],
        compiler_params=pltpu.CompilerParams(
            dimension_semantics=("arbitrary", "arbitrary"), vmem_limit_bytes=VMEM_LIMIT),
        name="diff_attention",
    )(slopes, lq1.reshape(1, QK_DIM), lk1.reshape(1, QK_DIM), lq2.reshape(1, QK_DIM),
      lk2.reshape(1, QK_DIM), subln_g.reshape(1, V_DIM), kb, db, k, qvt, qvt,
      *[w for w, _ in cast])


def _causal_conv3(u_sc, halo_ref, u, cw, first, rows):
    hist = halo_ref[...]
    u_sc[0:HALO, :] = jnp.where(first, jnp.zeros_like(hist), hist)
    u_sc[HALO:HALO + rows, :] = u
    halo_ref[...] = u[rows - HALO:rows, :]
    return (u * cw[2:3, :]
            + u_sc[HALO - 1:HALO - 1 + rows, :] * cw[1:2, :]
            + u_sc[HALO - 2:HALO - 2 + rows, :] * cw[0:1, :])


def _ffn_tile(x, first, gpre_ref, gpost_ref, wup_ref, cw_ref, wdn_ref,
              h_sc, u_sc, halo_sc, act_sc):
    rows = x.shape[0]
    c = FFN_CHUNK
    h_sc[...] = _rms(x, gpre_ref[...], NORM_EPS).astype(BF16)
    for j in range(D_FF // c):
        gs = slice(c * j, c * (j + 1))
        vs = slice(D_FF + c * j, D_FF + c * (j + 1))
        h = h_sc[...]
        u = jnp.concatenate(
            [jnp.dot(h, wup_ref[:, gs], preferred_element_type=F32),
             jnp.dot(h, wup_ref[:, vs], preferred_element_type=F32)], axis=1)
        cw = jnp.concatenate([cw_ref[:, gs], cw_ref[:, vs]], axis=1)
        y = _causal_conv3(u_sc, halo_sc.at[j], u, cw, first, rows)
        gate = y[:, 0:c]
        act_sc[:, gs] = (gate * jax.nn.sigmoid(gate) * y[:, c:2 * c]).astype(BF16)
    m = jnp.dot(act_sc[...], wdn_ref[...], preferred_element_type=F32)
    return x + _rms(m, gpost_ref[...], NORM_EPS)


def _ffn_kernel(x_ref, gpre_ref, gpost_ref, wup_ref, cw_ref, wdn_ref, o_ref, *scratch):
    first = (pl.program_id(0) % (SEQ // x_ref.shape[0])) == 0
    o_ref[...] = _ffn_tile(x_ref[...], first, gpre_ref, gpost_ref, wup_ref, cw_ref, wdn_ref,
                           *scratch)


def _oproj_ffn_kernel(a_ref, wo_ref, gmix_ref, x_ref, gpre_ref, gpost_ref, wup_ref, cw_ref,
                      wdn_ref, o_ref, *scratch):
    first = (pl.program_id(0) % (SEQ // x_ref.shape[0])) == 0
    mix = jnp.dot(a_ref[...], wo_ref[...], preferred_element_type=F32)
    x = x_ref[...] + _rms(mix, gmix_ref[...], NORM_EPS)
    o_ref[...] = _ffn_tile(x, first, gpre_ref, gpost_ref, wup_ref, cw_ref, wdn_ref, *scratch)


def _ffn(x2, g_pre, g_post, w_up, conv_w, w_down, attn=None):
    n_rows, d = x2.shape
    c = FFN_CHUNK
    row_spec = pl.BlockSpec((ROW_TILE, d), lambda i: (i, 0))
    in_specs = [
        row_spec,
        _resident((1, d)),
        _resident((1, d)),
        _resident((d, 2 * D_FF)),
        _resident((CONV_WIDTH, 2 * D_FF)),
        _resident((D_FF, d)),
    ]
    args = [x2, g_pre.reshape(1, d), g_post.reshape(1, d), w_up, conv_w, w_down]
    body = _ffn_kernel
    if attn is not None:
        a2, w_o, g_mix = attn
        in_specs = [pl.BlockSpec((ROW_TILE, a2.shape[1]), lambda i: (i, 0)),
                    _resident(w_o.shape), _resident((1, d))] + in_specs
        args = [a2, w_o, g_mix.reshape(1, d)] + args
        body = _oproj_ffn_kernel
    return pl.pallas_call(
        body,
        grid=(n_rows // ROW_TILE,),
        in_specs=in_specs,
        out_specs=row_spec,
        out_shape=jax.ShapeDtypeStruct((n_rows, d), F32),
        scratch_shapes=[
            pltpu.VMEM((ROW_TILE, d), BF16),
            pltpu.VMEM((HALO + ROW_TILE, 2 * c), F32),
            pltpu.VMEM((D_FF // c, HALO, 2 * c), F32),
            pltpu.VMEM((ROW_TILE, D_FF), BF16),
        ],
        compiler_params=pltpu.CompilerParams(
            dimension_semantics=("arbitrary",), vmem_limit_bytes=VMEM_LIMIT),
        name="conv_glu_ffn",
    )(*args)


def _mixer_kernel(x_ref, gpre_ref, gpost_ref, win_ref, cw_ref, wout_ref, o_ref,
                  h_sc, u_sc, halo_sc, y_sc):
    rows = x_ref.shape[0]
    c = MIX_CHUNK
    first = (pl.program_id(0) % (SEQ // rows)) == 0
    x = x_ref[...]
    h_sc[...] = _rms(x, gpre_ref[...], NORM_EPS).astype(BF16)
    for j in range(D_MODEL // c):
        cs = slice(c * j, c * (j + 1))
        h = h_sc[...]
        b_gate, c_gate, hv = (
            jnp.dot(h, win_ref[:, part * D_MODEL + c * j:part * D_MODEL + c * (j + 1)],
                    preferred_element_type=F32) for part in range(3))
        y = _causal_conv3(u_sc, halo_sc.at[j], c_gate * hv, cw_ref[:, cs], first, rows)
        y_sc[:, cs] = (b_gate * y).astype(BF16)
    m = jnp.dot(y_sc[...], wout_ref[...], preferred_element_type=F32)
    o_ref[...] = x + _rms(m, gpost_ref[...], NORM_EPS)


def _mixer(x2, g_pre, g_post, w_in, conv_w, w_out):
    n_rows, d = x2.shape
    c = MIX_CHUNK
    return pl.pallas_call(
        _mixer_kernel,
        grid=(n_rows // ROW_TILE,),
        in_specs=[
            pl.BlockSpec((ROW_TILE, d), lambda i: (i, 0)),
            _resident((1, d)),
            _resident((1, d)),
            _resident((d, 3 * d)),
            _resident((CONV_WIDTH, d)),
            _resident((d, d)),
        ],
        out_specs=pl.BlockSpec((ROW_TILE, d), lambda i: (i, 0)),
        out_shape=jax.ShapeDtypeStruct((n_rows, d), F32),
        scratch_shapes=[
            pltpu.VMEM((ROW_TILE, d), BF16),
            pltpu.VMEM((HALO + ROW_TILE, c), F32),
            pltpu.VMEM((d // c, HALO, c), F32),
            pltpu.VMEM((ROW_TILE, d), BF16),
        ],
        compiler_params=pltpu.CompilerParams(
            dimension_semantics=("arbitrary",), vmem_limit_bytes=VMEM_LIMIT),
        name="conv_mixer",
    )(x2, g_pre.reshape(1, d), g_post.reshape(1, d), w_in, conv_w, w_out)


def _qkv_layout(w):
    d = w.shape[0]
    qk = N_HEADS * QK_DIM
    q1, q2, k1, k2 = (w[:, i * qk:(i + 1) * qk].reshape(d, N_HEADS, QK_DIM) for i in range(4))
    w_k = jnp.concatenate([k1, k2], axis=-1).reshape(d, 2 * qk).astype(BF16)
    w_q = jnp.concatenate([q1, q2], axis=-1).reshape(d, 2 * qk)
    w_qvt = jnp.concatenate([w_q, w[:, 4 * qk:]], axis=1).T.astype(BF16)
    return w_k, w_qvt


def kernel(x, norm_g, attn_w_qkv, attn_w_o, attn_lambda_q1, attn_lambda_k1, attn_lambda_q2,
           attn_lambda_k2, attn_subln_g, conv_w_in, conv_w, conv_w_out, ffn_w_up, ffn_conv_w,
           ffn_w_down):
    b, s, d = x.shape
    assert (s, d) == (SEQ, D_MODEL) and norm_g.shape == (2, 4, d)
    assert attn_w_qkv.shape[0] == 1 and conv_w_in.shape[0] == 1 and ffn_w_up.shape[0] == 2
    slopes = jnp.exp2(-8.0 * jnp.arange(1, N_HEADS + 1, dtype=F32) / N_HEADS)
    x2 = x.reshape(b * s, d)

    g = norm_g[0]
    lambda_init = 0.8 - 0.6 * math.exp(-0.3 * 0)
    w_k, w_qvt = _qkv_layout(attn_w_qkv[0])
    k, qvt, w_o, w_up0, w_dn0 = _qkv_proj(
        x2.reshape(b, s, d), g[0], w_k, w_qvt,
        cast=[(attn_w_o, 0), (ffn_w_up, 0), (ffn_w_down, 0)])
    o, w_in, w_out, w_up1, w_dn1 = _attention(
        k, qvt, slopes, attn_lambda_q1[0], attn_lambda_k1[0], attn_lambda_q2[0],
        attn_lambda_k2[0], attn_subln_g[0], lambda_init,
        cast=[(conv_w_in, 0), (conv_w_out, 0), (ffn_w_up, 1), (ffn_w_down, 1)])
    x2 = _ffn(x2, g[2], g[3], w_up0, ffn_conv_w[0], w_dn0,
              attn=(o.reshape(b * s, d), w_o, g[1]))

    g = norm_g[1]
    x2 = _mixer(x2, g[0], g[1], w_in, conv_w[0], w_out)
    x2 = _ffn(x2, g[2], g[3], w_up1, ffn_conv_w[1], w_dn1)
    return x2.reshape(b, s, d)
```

```python
import functools
import math

import jax
import jax.numpy as jnp
from jax import lax
from jax.experimental import pallas as pl
from jax.experimental.pallas import tpu as pltpu

D_MODEL = 1024
SEQ = 2048
CHUNK = 64
N_HEADS = 8
QK_DIM = 64
V_DIM = 128
CONV_WIDTH = 3
D_FF = 2816
NORM_EPS = 1e-6
SUBLN_EPS = 1e-5

F32 = jnp.float32
BF16 = jnp.bfloat16

ROW_TILE = 512
FFN_ROWS = 1024
ATT_TILE = 256
ATT_HEADS = 2
SCORE_LEAD = 4
S_SLOTS = SCORE_LEAD + 1
BF16_SUBLANES = 16
ONES_ROWS = BF16_SUBLANES
FFN_CHUNK = 256
MIX_CHUNK = 256
HALO = 8
VMEM_LIMIT = 56 * 1024 * 1024
MASK_VALUE = -1e30
LOG2E = math.log2(math.e)

_NT = (((1,), (1,)), ((), ()))


def _rms(x, g, eps):
    ms = jnp.mean(x * x, axis=-1, keepdims=True)
    return x * lax.rsqrt(ms + eps) * g


def _resident(shape):
    nd = len(shape)
    return pl.BlockSpec(shape, lambda *_: (0,) * nd, pipeline_mode=pl.Buffered(1))


def _cast_job(w, layer, n_steps, step_of):
    rows, cols = w.shape[-2:]
    n_blocks = n_steps
    while rows % n_blocks or (rows // n_blocks) % BF16_SUBLANES:
        n_blocks //= 2
    every = n_steps // n_blocks
    rb = rows // n_blocks
    in_spec = pl.BlockSpec((None, rb, cols), lambda *g: (layer, step_of(*g) // every, 0))
    out_spec = pl.BlockSpec((rb, cols), lambda *g: (step_of(*g) // every, 0))
    return in_spec, out_spec, jax.ShapeDtypeStruct((rows, cols), BF16)


def _run_cast_jobs(srcs, dsts):
    for src, dst in zip(srcs, dsts):
        dst[...] = src[...].astype(BF16)


def _qkv_proj_kernel(x_ref, g_ref, wk_ref, wqvt_ref, *refs, n_cast):
    cast_srcs, (k_ref, qvt_ref), cast_dsts = refs[:n_cast], refs[n_cast:n_cast + 2], refs[n_cast + 2:]
    h = _rms(x_ref[...], g_ref[...], NORM_EPS).astype(BF16)
    k_ref[...] = jnp.dot(h, wk_ref[...], preferred_element_type=F32).astype(BF16)
    qvt = lax.dot_general(wqvt_ref[...], h, _NT, preferred_element_type=F32)
    n_q = N_HEADS * 2 * QK_DIM
    qvt_ref[0:n_q, :] = (qvt[0:n_q] * (LOG2E * QK_DIM ** -0.5)).astype(BF16)
    qvt_ref[n_q:, :] = qvt[n_q:].astype(BF16)
    _run_cast_jobs(cast_srcs, cast_dsts)


def _qkv_proj(x3, g, w_k, w_qvt, cast):
    b, s, d = x3.shape
    tiles = s // ROW_TILE
    n_k, n_qv = w_k.shape[1], w_qvt.shape[0]
    jobs = [_cast_job(w, layer, b * tiles, lambda bi, i: bi * tiles + i) for w, layer in cast]
    return pl.pallas_call(
        functools.partial(_qkv_proj_kernel, n_cast=len(jobs)),
        grid=(b, tiles),
        in_specs=[
            pl.BlockSpec((None, ROW_TILE, d), lambda bi, i: (bi, i, 0)),
            _resident((1, d)),
            _resident((d, n_k)),
            _resident((n_qv, d)),
        ] + [job[0] for job in jobs],
        out_specs=[
            pl.BlockSpec((None, ROW_TILE, n_k), lambda bi, i: (bi, i, 0)),
            pl.BlockSpec((None, n_qv, ROW_TILE), lambda bi, i: (bi, 0, i)),
        ] + [job[1] for job in jobs],
        out_shape=[
            jax.ShapeDtypeStruct((b, s, n_k), BF16),
            jax.ShapeDtypeStruct((b, n_qv, s), BF16),
        ] + [job[2] for job in jobs],
        compiler_params=pltpu.CompilerParams(
            dimension_semantics=("arbitrary", "arbitrary"), vmem_limit_bytes=VMEM_LIMIT),
        name="qkv_proj",
    )(x3, g.reshape(1, d), w_k, w_qvt, *[w for w, _ in cast])


def _attn_kernel(slope_ref, lq1_ref, lk1_ref, lq2_ref, lk2_ref, g_ref, kb_ref, db_ref, k_ref,
                 qt_ref, vt_ref, *refs, lambda_init, n_cast):
    cast_srcs, o_ref, cast_dsts = refs[:n_cast], refs[n_cast], refs[n_cast + 1:2 * n_cast + 1]
    scratch = refs[2 * n_cast + 1:]
    lam = (jnp.exp(jnp.sum(lq1_ref[...] * lk1_ref[...], axis=-1, keepdims=True))
           - jnp.exp(jnp.sum(lq2_ref[...] * lk2_ref[...], axis=-1, keepdims=True))
           + lambda_init)
    for hd in range(ATT_HEADS):
        _attn_head(slope_ref[pl.program_id(1) * ATT_HEADS + hd], lam, g_ref,
                   kb_ref.at[hd], db_ref.at[hd],
                   k_ref.at[:, hd * 2 * QK_DIM:(hd + 1) * 2 * QK_DIM],
                   qt_ref.at[hd * 2 * QK_DIM:(hd + 1) * 2 * QK_DIM, :],
                   vt_ref.at[hd * V_DIM:(hd + 1) * V_DIM, :],
                   o_ref.at[:, hd * V_DIM:(hd + 1) * V_DIM],
                   *[sc.at[hd] for sc in scratch], lambda_init=lambda_init)
    _run_cast_jobs(cast_srcs, cast_dsts)


def _attn_head(slope, lam, g_ref, kb_sc, db_sc, k_ref, qt_ref, vt_ref, o_ref, qs_sc, acc_sc,
               s_sc, *, lambda_init):
    t = ATT_TILE
    nq = SEQ // t
    slope = slope * LOG2E
    row = lax.broadcasted_iota(jnp.int32, (2 * QK_DIM, t), 0)

    for qi in range(nq):
        qt = qt_ref[:, qi * t:(qi + 1) * t]
        zero = jnp.zeros_like(qt)
        qs_sc[qi, 0] = jnp.where(row < QK_DIM, qt, zero)
        qs_sc[qi, 1] = jnp.where(row >= QK_DIM, qt, zero)

    tiles = [(kj, qi, mp) for kj in range(nq) for qi in range(kj, nq) for mp in range(2)]

    def emit_scores(idx):
        kj, qi, mp = tiles[idx]
        k_tile = k_ref[kj * t:(kj + 1) * t, :]
        bias = db_sc[...] if qi == kj else kb_sc[...]
        s = jnp.dot(k_tile, qs_sc[qi, mp], preferred_element_type=F32) + bias
        s_sc[idx % S_SLOTS] = s
        s_max[idx] = jnp.max(s, axis=0, keepdims=True)

    s_max = [None] * len(tiles)
    for idx in range(min(SCORE_LEAD, len(tiles))):
        emit_scores(idx)

    ones_rows = jnp.ones((ONES_ROWS, t), BF16)
    m = [[None, None] for _ in range(nq)]
    for idx, (kj, qi, mp) in enumerate(tiles):
        if idx + SCORE_LEAD < len(tiles):
            emit_scores(idx + SCORE_LEAD)
        s = s_sc[idx % S_SLOTS]
        v_ext = jnp.concatenate([vt_ref[:, kj * t:(kj + 1) * t], ones_rows], axis=0)
        shift = slope * jnp.full((1, t), float((qi - kj) * t), F32)
        m_tile = s_max[idx] - shift
        if kj == 0:
            m_new = m_tile
            p = jnp.exp2((s - (m_new + shift)).astype(BF16))
            acc_sc[qi, mp] = jnp.dot(v_ext, p, preferred_element_type=F32)
        else:
            m_new = jnp.maximum(m[qi][mp], m_tile)
            alpha = jnp.exp2(m[qi][mp] - m_new)
            p = jnp.exp2((s - (m_new + shift)).astype(BF16))
            acc_sc[qi, mp] = alpha * acc_sc[qi, mp] + jnp.dot(
                v_ext, p, preferred_element_type=F32)
        m[qi][mp] = m_new
        if qi == kj and mp == 1:
            acc1 = acc_sc[qi, 0]
            acc2 = acc_sc[qi, 1]
            o_t = (acc1[0:V_DIM] / acc1[V_DIM:V_DIM + 1]
                   - lam * (acc2[0:V_DIM] / acc2[V_DIM:V_DIM + 1]))
            ms = jnp.mean(o_t * o_t, axis=0, keepdims=True)
            o = (o_t * lax.rsqrt(ms + SUBLN_EPS)).T
            o = o * g_ref[...] * (1.0 - lambda_init)
            o_ref[qi * t:(qi + 1) * t, :] = o.astype(o_ref.dtype)


def _attention(k, qvt, slopes, lq1, lk1, lq2, lk2, subln_g, lambda_init, cast):
    b = k.shape[0]
    t = ATT_TILE
    smem = pl.BlockSpec(memory_space=pltpu.SMEM)
    lam_spec = _resident((1, QK_DIM))
    hd = ATT_HEADS
    steps = N_HEADS // hd
    jobs = [_cast_job(w, layer, b * steps, lambda bi, h: bi * steps + h) for w, layer in cast]
    key = jnp.arange(t, dtype=F32)[:, None]
    qry = jnp.arange(t, dtype=F32)[None, :]
    slope2 = (slopes * LOG2E)[:, None, None]
    kb = slope2 * jnp.broadcast_to(key, (t, t))
    db = jnp.where((key // CHUNK) <= (qry // CHUNK), slope2 * (qry - jnp.abs(qry - key)),
                   MASK_VALUE)
    table_spec = pl.BlockSpec((hd, t, t), lambda bi, h: (h, 0, 0))
    return pl.pallas_call(
        functools.partial(_attn_kernel, lambda_init=lambda_init, n_cast=len(jobs)),
        grid=(b, steps),
        in_specs=[
            smem, lam_spec, lam_spec, lam_spec, lam_spec, _resident((1, V_DIM)),
            table_spec, table_spec,
            pl.BlockSpec((None, SEQ, hd * 2 * QK_DIM), lambda bi, h: (bi, 0, h)),
            pl.BlockSpec((None, hd * 2 * QK_DIM, SEQ), lambda bi, h: (bi, h, 0)),
            pl.BlockSpec((None, hd * V_DIM, SEQ), lambda bi, h: (bi, steps + h, 0)),
        ] + [job[0] for job in jobs],
        out_specs=[pl.BlockSpec((None, SEQ, hd * V_DIM), lambda bi, h: (bi, 0, h))]
        + [job[1] for job in jobs],
        out_shape=[jax.ShapeDtypeStruct((b, SEQ, N_HEADS * V_DIM), BF16)]
        + [job[2] for job in jobs],
        scratch_shapes=[
            pltpu.VMEM((hd, SEQ // t, 2, 2 * QK_DIM, t), BF16),
            pltpu.VMEM((hd, SEQ // t, 2, V_DIM + ONES_ROWS, t), F32),
            pltpu.VMEM((hd, S_SLOTS, t, t), F32),
        ],
        compiler_params=pltpu.CompilerParams(
            dimension_semantics=("arbitrary", "arbitrary"), vmem_limit_bytes=VMEM_LIMIT),
        name="diff_attention",
    )(slopes, lq1.reshape(1, QK_DIM), lk1.reshape(1, QK_DIM), lq2.reshape(1, QK_DIM),
      lk2.reshape(1, QK_DIM), subln_g.reshape(1, V_DIM), kb, db, k, qvt, qvt,
      *[w for w, _ in cast])


def _causal_conv3(u_sc, halo_ref, u, cw, first, rows):
    hist = halo_ref[...]
    u_sc[0:HALO, :] = jnp.where(first, jnp.zeros_like(hist), hist)
    u_sc[HALO:HALO + rows, :] = u
    halo_ref[...] = u[rows - HALO:rows, :]
    return (u * cw[2:3, :]
            + u_sc[HALO - 1:HALO - 1 + rows, :] * cw[1:2, :]
            + u_sc[HALO - 2:HALO - 2 + rows, :] * cw[0:1, :])


def _ffn_tile(x, first, gpre_ref, gpost_ref, wup_ref, cw_ref, wdn_ref,
              h_sc, u_sc, halo_sc, act_sc):
    rows = x.shape[0]
    c = FFN_CHUNK
    h_sc[...] = _rms(x, gpre_ref[...], NORM_EPS).astype(BF16)
    for j in range(D_FF // c):
        gs = slice(c * j, c * (j + 1))
        vs = slice(D_FF + c * j, D_FF + c * (j + 1))
        h = h_sc[...]
        u = jnp.concatenate(
            [jnp.dot(h, wup_ref[:, gs], preferred_element_type=F32),
             jnp.dot(h, wup_ref[:, vs], preferred_element_type=F32)], axis=1)
        cw = jnp.concatenate([cw_ref[:, gs], cw_ref[:, vs]], axis=1)
        y = _causal_conv3(u_sc, halo_sc.at[j], u, cw, first, rows)
        gate = y[:, 0:c]
        act_sc[:, gs] = (gate * jax.nn.sigmoid(gate) * y[:, c:2 * c]).astype(BF16)
    m = jnp.dot(act_sc[...], wdn_ref[...], preferred_element_type=F32)
    return x + _rms(m, gpost_ref[...], NORM_EPS)


def _ffn_kernel(x_ref, gpre_ref, gpost_ref, wup_ref, cw_ref, wdn_ref, o_ref, *scratch):
    first = (pl.program_id(0) % (SEQ // x_ref.shape[0])) == 0
    o_ref[...] = _ffn_tile(x_ref[...], first, gpre_ref, gpost_ref, wup_ref, cw_ref, wdn_ref,
                           *scratch)


def _oproj_ffn_kernel(a_ref, wo_ref, gmix_ref, x_ref, gpre_ref, gpost_ref, wup_ref, cw_ref,
                      wdn_ref, o_ref, *scratch):
    first = (pl.program_id(0) % (SEQ // x_ref.shape[0])) == 0
    mix = jnp.dot(a_ref[...], wo_ref[...], preferred_element_type=F32)
    x = x_ref[...] + _rms(mix, gmix_ref[...], NORM_EPS)
    o_ref[...] = _ffn_tile(x, first, gpre_ref, gpost_ref, wup_ref, cw_ref, wdn_ref, *scratch)


def _ffn(x2, g_pre, g_post, w_up, conv_w, w_down, attn=None):
    n_rows, d = x2.shape
    c = FFN_CHUNK
    rows = FFN_ROWS
    row_spec = pl.BlockSpec((rows, d), lambda i: (i, 0))
    in_specs = [
        row_spec,
        _resident((1, d)),
        _resident((1, d)),
        _resident((d, 2 * D_FF)),
        _resident((CONV_WIDTH, 2 * D_FF)),
        _resident((D_FF, d)),
    ]
    args = [x2, g_pre.reshape(1, d), g_post.reshape(1, d), w_up, conv_w, w_down]
    body = _ffn_kernel
    if attn is not None:
        a2, w_o, g_mix = attn
        in_specs = [pl.BlockSpec((rows, a2.shape[1]), lambda i: (i, 0)),
                    _resident(w_o.shape), _resident((1, d))] + in_specs
        args = [a2, w_o, g_mix.reshape(1, d)] + args
        body = _oproj_ffn_kernel
    return pl.pallas_call(
        body,
        grid=(n_rows // rows,),
        in_specs=in_specs,
        out_specs=row_spec,
        out_shape=jax.ShapeDtypeStruct((n_rows, d), F32),
        scratch_shapes=[
            pltpu.VMEM((rows, d), BF16),
            pltpu.VMEM((HALO + rows, 2 * c), F32),
            pltpu.VMEM((D_FF // c, HALO, 2 * c), F32),
            pltpu.VMEM((rows, D_FF), BF16),
        ],
        compiler_params=pltpu.CompilerParams(
            dimension_semantics=("arbitrary",), vmem_limit_bytes=VMEM_LIMIT),
        name="conv_glu_ffn",
    )(*args)


def _mixer_kernel(x_ref, gpre_ref, gpost_ref, win_ref, cw_ref, wout_ref, o_ref,
                  h_sc, u_sc, halo_sc, y_sc):
    rows = x_ref.shape[0]
    c = MIX_CHUNK
    first = (pl.program_id(0) % (SEQ // rows)) == 0
    x = x_ref[...]
    h_sc[...] = _rms(x, gpre_ref[...], NORM_EPS).astype(BF16)
    for j in range(D_MODEL // c):
        cs = slice(c * j, c * (j + 1))
        h = h_sc[...]
        b_gate, c_gate, hv = (
            jnp.dot(h, win_ref[:, part * D_MODEL + c * j:part * D_MODEL + c * (j + 1)],
                    preferred_element_type=F32) for part in range(3))
        y = _causal_conv3(u_sc, halo_sc.at[j], c_gate * hv, cw_ref[:, cs], first, rows)
        y_sc[:, cs] = (b_gate * y).astype(BF16)
    m = jnp.dot(y_sc[...], wout_ref[...], preferred_element_type=F32)
    o_ref[...] = x + _rms(m, gpost_ref[...], NORM_EPS)


def _mixer(x2, g_pre, g_post, w_in, conv_w, w_out):
    n_rows, d = x2.shape
    c = MIX_CHUNK
    return pl.pallas_call(
        _mixer_kernel,
        grid=(n_rows // ROW_TILE,),
        in_specs=[
            pl.BlockSpec((ROW_TILE, d), lambda i: (i, 0)),
            _resident((1, d)),
            _resident((1, d)),
            _resident((d, 3 * d)),
            _resident((CONV_WIDTH, d)),
            _resident((d, d)),
        ],
        out_specs=pl.BlockSpec((ROW_TILE, d), lambda i: (i, 0)),
        out_shape=jax.ShapeDtypeStruct((n_rows, d), F32),
        scratch_shapes=[
            pltpu.VMEM((ROW_TILE, d), BF16),
            pltpu.VMEM((HALO + ROW_TILE, c), F32),
            pltpu.VMEM((d // c, HALO, c), F32),
            pltpu.VMEM((ROW_TILE, d), BF16),
        ],
        compiler_params=pltpu.CompilerParams(
            dimension_semantics=("arbitrary",), vmem_limit_bytes=VMEM_LIMIT),
        name="conv_mixer",
    )(x2, g_pre.reshape(1, d), g_post.reshape(1, d), w_in, conv_w, w_out)


def _qkv_layout(w):
    d = w.shape[0]
    qk = N_HEADS * QK_DIM
    q1, q2, k1, k2 = (w[:, i * qk:(i + 1) * qk].reshape(d, N_HEADS, QK_DIM) for i in range(4))
    w_k = jnp.concatenate([k1, k2], axis=-1).reshape(d, 2 * qk).astype(BF16)
    w_q = jnp.concatenate([q1, q2], axis=-1).reshape(d, 2 * qk)
    w_qvt = jnp.concatenate([w_q, w[:, 4 * qk:]], axis=1).T.astype(BF16)
    return w_k, w_qvt


def kernel(x, norm_g, attn_w_qkv, attn_w_o, attn_lambda_q1, attn_lambda_k1, attn_lambda_q2,
           attn_lambda_k2, attn_subln_g, conv_w_in, conv_w, conv_w_out, ffn_w_up, ffn_conv_w,
           ffn_w_down):
    b, s, d = x.shape
    assert (s, d) == (SEQ, D_MODEL) and norm_g.shape == (2, 4, d)
    assert attn_w_qkv.shape[0] == 1 and conv_w_in.shape[0] == 1 and ffn_w_up.shape[0] == 2
    slopes = jnp.exp2(-8.0 * jnp.arange(1, N_HEADS + 1, dtype=F32) / N_HEADS)
    x2 = x.reshape(b * s, d)

    g = norm_g[0]
    lambda_init = 0.8 - 0.6 * math.exp(-0.3 * 0)
    w_k, w_qvt = _qkv_layout(attn_w_qkv[0])
    k, qvt, w_o, w_up0, w_dn0 = _qkv_proj(
        x2.reshape(b, s, d), g[0], w_k, w_qvt,
        cast=[(attn_w_o, 0), (ffn_w_up, 0), (ffn_w_down, 0)])
    o, w_in, w_out, w_up1, w_dn1 = _attention(
        k, qvt, slopes, attn_lambda_q1[0], attn_lambda_k1[0], attn_lambda_q2[0],
        attn_lambda_k2[0], attn_subln_g[0], lambda_init,
        cast=[(conv_w_in, 0), (conv_w_out, 0), (ffn_w_up, 1), (ffn_w_down, 1)])
    x2 = _ffn(x2, g[2], g[3], w_up0, ffn_conv_w[0], w_dn0,
              attn=(o.reshape(b * s, d), w_o, g[1]))

    g = norm_g[1]
    x2 = _mixer(x2, g[0], g[1], w_in, conv_w[0], w_out)
    x2 = _ffn(x2, g[2], g[3], w_up1, ffn_conv_w[1], w_dn1)
    return x2.reshape(b, s, d)
```

```python
import functools
import math

import jax
import jax.numpy as jnp
from jax import lax
from jax.experimental import pallas as pl
from jax.experimental.pallas import tpu as pltpu

D_MODEL = 1024
SEQ = 2048
CHUNK = 64
N_HEADS = 8
QK_DIM = 64
V_DIM = 128
CONV_WIDTH = 3
D_FF = 2816
NORM_EPS = 1e-6
SUBLN_EPS = 1e-5

F32 = jnp.float32
BF16 = jnp.bfloat16

ROW_TILE = 512
FFN_ROWS = 512
ATT_TILE = 256
ATT_HEADS = 4
SCORE_LEAD = 4
S_SLOTS = SCORE_LEAD + 1
BF16_SUBLANES = 16
ONES_ROWS = BF16_SUBLANES
FFN_CHUNK = 256
MIX_CHUNK = 256
HALO = 8
VMEM_LIMIT = 56 * 1024 * 1024
MASK_VALUE = -1e30
LOG2E = math.log2(math.e)

_NT = (((1,), (1,)), ((), ()))


def _rms(x, g, eps):
    ms = jnp.mean(x * x, axis=-1, keepdims=True)
    return x * lax.rsqrt(ms + eps) * g


def _resident(shape):
    nd = len(shape)
    return pl.BlockSpec(shape, lambda *_: (0,) * nd, pipeline_mode=pl.Buffered(1))


def _cast_job(w, layer, n_steps, step_of):
    rows, cols = w.shape[-2:]
    n_blocks = n_steps
    while rows % n_blocks or (rows // n_blocks) % BF16_SUBLANES:
        n_blocks //= 2
    every = n_steps // n_blocks
    rb = rows // n_blocks
    in_spec = pl.BlockSpec((None, rb, cols), lambda *g: (layer, step_of(*g) // every, 0))
    out_spec = pl.BlockSpec((rb, cols), lambda *g: (step_of(*g) // every, 0))
    return in_spec, out_spec, jax.ShapeDtypeStruct((rows, cols), BF16)


def _run_cast_jobs(srcs, dsts):
    for src, dst in zip(srcs, dsts):
        dst[...] = src[...].astype(BF16)


def _qkv_proj_kernel(x_ref, g_ref, wk_ref, wqvt_ref, *refs, n_cast):
    cast_srcs, (k_ref, qvt_ref), cast_dsts = refs[:n_cast], refs[n_cast:n_cast + 2], refs[n_cast + 2:]
    h = _rms(x_ref[...], g_ref[...], NORM_EPS).astype(BF16)
    k_ref[...] = jnp.dot(h, wk_ref[...], preferred_element_type=F32).astype(BF16)
    qvt = lax.dot_general(wqvt_ref[...], h, _NT, preferred_element_type=F32)
    n_q = N_HEADS * 2 * QK_DIM
    qvt_ref[0:n_q, :] = (qvt[0:n_q] * (LOG2E * QK_DIM ** -0.5)).astype(BF16)
    qvt_ref[n_q:, :] = qvt[n_q:].astype(BF16)
    _run_cast_jobs(cast_srcs, cast_dsts)


def _qkv_proj(x3, g, w_k, w_qvt, cast):
    b, s, d = x3.shape
    tiles = s // ROW_TILE
    n_k, n_qv = w_k.shape[1], w_qvt.shape[0]
    jobs = [_cast_job(w, layer, b * tiles, lambda bi, i: bi * tiles + i) for w, layer in cast]
    return pl.pallas_call(
        functools.partial(_qkv_proj_kernel, n_cast=len(jobs)),
        grid=(b, tiles),
        in_specs=[
            pl.BlockSpec((None, ROW_TILE, d), lambda bi, i: (bi, i, 0)),
            _resident((1, d)),
            _resident((d, n_k)),
            _resident((n_qv, d)),
        ] + [job[0] for job in jobs],
        out_specs=[
            pl.BlockSpec((None, ROW_TILE, n_k), lambda bi, i: (bi, i, 0)),
            pl.BlockSpec((None, n_qv, ROW_TILE), lambda bi, i: (bi, 0, i)),
        ] + [job[1] for job in jobs],
        out_shape=[
            jax.ShapeDtypeStruct((b, s, n_k), BF16),
            jax.ShapeDtypeStruct((b, n_qv, s), BF16),
        ] + [job[2] for job in jobs],
        compiler_params=pltpu.CompilerParams(
            dimension_semantics=("arbitrary", "arbitrary"), vmem_limit_bytes=VMEM_LIMIT),
        name="qkv_proj",
    )(x3, g.reshape(1, d), w_k, w_qvt, *[w for w, _ in cast])


def _attn_kernel(slope_ref, lq1_ref, lk1_ref, lq2_ref, lk2_ref, g_ref, kb_ref, db_ref, k_ref,
                 qt_ref, vt_ref, *refs, lambda_init, n_cast):
    cast_srcs, o_ref, cast_dsts = refs[:n_cast], refs[n_cast], refs[n_cast + 1:2 * n_cast + 1]
    scratch = refs[2 * n_cast + 1:]
    lam = (jnp.exp(jnp.sum(lq1_ref[...] * lk1_ref[...], axis=-1, keepdims=True))
           - jnp.exp(jnp.sum(lq2_ref[...] * lk2_ref[...], axis=-1, keepdims=True))
           + lambda_init)
    for hd in range(ATT_HEADS):
        _attn_head(slope_ref[pl.program_id(1) * ATT_HEADS + hd], lam, g_ref,
                   kb_ref.at[hd], db_ref.at[hd],
                   k_ref.at[:, hd * 2 * QK_DIM:(hd + 1) * 2 * QK_DIM],
                   qt_ref.at[hd * 2 * QK_DIM:(hd + 1) * 2 * QK_DIM, :],
                   vt_ref.at[hd * V_DIM:(hd + 1) * V_DIM, :],
                   o_ref.at[:, hd * V_DIM:(hd + 1) * V_DIM],
                   *[sc.at[hd] for sc in scratch], lambda_init=lambda_init)
    _run_cast_jobs(cast_srcs, cast_dsts)


def _attn_head(slope, lam, g_ref, kb_sc, db_sc, k_ref, qt_ref, vt_ref, o_ref, qs_sc, acc_sc,
               s_sc, *, lambda_init):
    t = ATT_TILE
    nq = SEQ // t
    slope = slope * LOG2E
    row = lax.broadcasted_iota(jnp.int32, (2 * QK_DIM, t), 0)

    for qi in range(nq):
        qt = qt_ref[:, qi * t:(qi + 1) * t]
        zero = jnp.zeros_like(qt)
        qs_sc[qi, 0] = jnp.where(row < QK_DIM, qt, zero)
        qs_sc[qi, 1] = jnp.where(row >= QK_DIM, qt, zero)

    tiles = [(kj, qi, mp) for kj in range(nq) for qi in range(kj, nq) for mp in range(2)]

    def emit_scores(idx):
        kj, qi, mp = tiles[idx]
        k_tile = k_ref[kj * t:(kj + 1) * t, :]
        bias = db_sc[...] if qi == kj else kb_sc[...]
        s = jnp.dot(k_tile, qs_sc[qi, mp], preferred_element_type=F32) + bias
        s_sc[idx % S_SLOTS] = s
        s_max[idx] = jnp.max(s, axis=0, keepdims=True)

    s_max = [None] * len(tiles)
    for idx in range(min(SCORE_LEAD, len(tiles))):
        emit_scores(idx)

    ones_rows = jnp.ones((ONES_ROWS, t), BF16)
    m = [[None, None] for _ in range(nq)]
    for idx, (kj, qi, mp) in enumerate(tiles):
        if idx + SCORE_LEAD < len(tiles):
            emit_scores(idx + SCORE_LEAD)
        s = s_sc[idx % S_SLOTS]
        v_ext = jnp.concatenate([vt_ref[:, kj * t:(kj + 1) * t], ones_rows], axis=0)
        shift = slope * jnp.full((1, t), float((qi - kj) * t), F32)
        m_tile = s_max[idx] - shift
        if kj == 0:
            m_new = m_tile
            p = jnp.exp2((s - (m_new + shift)).astype(BF16))
            acc_sc[qi, mp] = jnp.dot(v_ext, p, preferred_element_type=F32)
        else:
            m_new = jnp.maximum(m[qi][mp], m_tile)
            alpha = jnp.exp2(m[qi][mp] - m_new)
            p = jnp.exp2((s - (m_new + shift)).astype(BF16))
            acc_sc[qi, mp] = alpha * acc_sc[qi, mp] + jnp.dot(
                v_ext, p, preferred_element_type=F32)
        m[qi][mp] = m_new
        if qi == kj and mp == 1:
            acc1 = acc_sc[qi, 0]
            acc2 = acc_sc[qi, 1]
            o_t = (acc1[0:V_DIM] / acc1[V_DIM:V_DIM + 1]
                   - lam * (acc2[0:V_DIM] / acc2[V_DIM:V_DIM + 1]))
            ms = jnp.mean(o_t * o_t, axis=0, keepdims=True)
            o = (o_t * lax.rsqrt(ms + SUBLN_EPS)).T
            o = o * g_ref[...] * (1.0 - lambda_init)
            o_ref[qi * t:(qi + 1) * t, :] = o.astype(o_ref.dtype)


def _attention(k, qvt, slopes, lq1, lk1, lq2, lk2, subln_g, lambda_init, cast):
    b = k.shape[0]
    t = ATT_TILE
    smem = pl.BlockSpec(memory_space=pltpu.SMEM)
    lam_spec = _resident((1, QK_DIM))
    hd = ATT_HEADS
    steps = N_HEADS // hd
    jobs = [_cast_job(w, layer, b * steps, lambda bi, h: bi * steps + h) for w, layer in cast]
    key = jnp.arange(t, dtype=F32)[:, None]
    qry = jnp.arange(t, dtype=F32)[None, :]
    slope2 = (slopes * LOG2E)[:, None, None]
    kb = slope2 * jnp.broadcast_to(key, (t, t))
    db = jnp.where((key // CHUNK) <= (qry // CHUNK), slope2 * (qry - jnp.abs(qry - key)),
                   MASK_VALUE)
    table_spec = pl.BlockSpec((hd, t, t), lambda bi, h: (h, 0, 0))
    return pl.pallas_call(
        functools.partial(_attn_kernel, lambda_init=lambda_init, n_cast=len(jobs)),
        grid=(b, steps),
        in_specs=[
            smem, lam_spec, lam_spec, lam_spec, lam_spec, _resident((1, V_DIM)),
            table_spec, table_spec,
            pl.BlockSpec((None, SEQ, hd * 2 * QK_DIM), lambda bi, h: (bi, 0, h)),
            pl.BlockSpec((None, hd * 2 * QK_DIM, SEQ), lambda bi, h: (bi, h, 0)),
            pl.BlockSpec((None, hd * V_DIM, SEQ), lambda bi, h: (bi, steps + h, 0)),
        ] + [job[0] for job in jobs],
        out_specs=[pl.BlockSpec((None, SEQ, hd * V_DIM), lambda bi, h: (bi, 0, h))]
        + [job[1] for job in jobs],
        out_shape=[jax.ShapeDtypeStruct((b, SEQ, N_HEADS * V_DIM), BF16)]
        + [job[2] for job in jobs],
        scratch_shapes=[
            pltpu.VMEM((hd, SEQ // t, 2, 2 * QK_DIM, t), BF16),
            pltpu.VMEM((hd, SEQ // t, 2, V_DIM + ONES_ROWS, t), F32),
            pltpu.VMEM((hd, S_SLOTS, t, t), F32),
        ],
        compiler_params=pltpu.CompilerParams(
            dimension_semantics=("arbitrary", "arbitrary"), vmem_limit_bytes=VMEM_LIMIT),
        name="diff_attention",
    )(slopes, lq1.reshape(1, QK_DIM), lk1.reshape(1, QK_DIM), lq2.reshape(1, QK_DIM),
      lk2.reshape(1, QK_DIM), subln_g.reshape(1, V_DIM), kb, db, k, qvt, qvt,
      *[w for w, _ in cast])


def _causal_conv3(u_sc, halo_ref, u, cw, first, rows):
    hist = halo_ref[...]
    u_sc[0:HALO, :] = jnp.where(first, jnp.zeros_like(hist), hist)
    u_sc[HALO:HALO + rows, :] = u
    halo_ref[...] = u[rows - HALO:rows, :]
    return (u * cw[2:3, :]
            + u_sc[HALO - 1:HALO - 1 + rows, :] * cw[1:2, :]
            + u_sc[HALO - 2:HALO - 2 + rows, :] * cw[0:1, :])


def _ffn_tile(x, first, gpre_ref, gpost_ref, wup_ref, cw_ref, wdn_ref,
              h_sc, u_sc, halo_sc, act_sc):
    rows = x.shape[0]
    c = FFN_CHUNK
    h_sc[...] = _rms(x, gpre_ref[...], NORM_EPS).astype(BF16)
    for j in range(D_FF // c):
        gs = slice(c * j, c * (j + 1))
        vs = slice(D_FF + c * j, D_FF + c * (j + 1))
        h = h_sc[...]
        u = jnp.concatenate(
            [jnp.dot(h, wup_ref[:, gs], preferred_element_type=F32),
             jnp.dot(h, wup_ref[:, vs], preferred_element_type=F32)], axis=1)
        cw = jnp.concatenate([cw_ref[:, gs], cw_ref[:, vs]], axis=1)
        y = _causal_conv3(u_sc, halo_sc.at[j], u, cw, first, rows)
        gate = y[:, 0:c]
        act_sc[:, gs] = (gate * jax.nn.sigmoid(gate) * y[:, c:2 * c]).astype(BF16)
    m = jnp.dot(act_sc[...], wdn_ref[...], preferred_element_type=F32)
    return x + _rms(m, gpost_ref[...], NORM_EPS)


def _ffn_kernel(x_ref, gpre_ref, gpost_ref, wup_ref, cw_ref, wdn_ref, o_ref, *scratch):
    first = (pl.program_id(0) % (SEQ // x_ref.shape[0])) == 0
    o_ref[...] = _ffn_tile(x_ref[...], first, gpre_ref, gpost_ref, wup_ref, cw_ref, wdn_ref,
                           *scratch)


def _oproj_ffn_kernel(a_ref, wo_ref, gmix_ref, x_ref, gpre_ref, gpost_ref, wup_ref, cw_ref,
                      wdn_ref, o_ref, *scratch):
    first = (pl.program_id(0) % (SEQ // x_ref.shape[0])) == 0
    mix = jnp.dot(a_ref[...], wo_ref[...], preferred_element_type=F32)
    x = x_ref[...] + _rms(mix, gmix_ref[...], NORM_EPS)
    o_ref[...] = _ffn_tile(x, first, gpre_ref, gpost_ref, wup_ref, cw_ref, wdn_ref, *scratch)


def _ffn(x2, g_pre, g_post, w_up, conv_w, w_down, attn=None):
    n_rows, d = x2.shape
    c = FFN_CHUNK
    rows = FFN_ROWS
    row_spec = pl.BlockSpec((rows, d), lambda i: (i, 0))
    in_specs = [
        row_spec,
        _resident((1, d)),
        _resident((1, d)),
        _resident((d, 2 * D_FF)),
        _resident((CONV_WIDTH, 2 * D_FF)),
        _resident((D_FF, d)),
    ]
    args = [x2, g_pre.reshape(1, d), g_post.reshape(1, d), w_up, conv_w, w_down]
    body = _ffn_kernel
    if attn is not None:
        a2, w_o, g_mix = attn
        in_specs = [pl.BlockSpec((rows, a2.shape[1]), lambda i: (i, 0)),
                    _resident(w_o.shape), _resident((1, d))] + in_specs
        args = [a2, w_o, g_mix.reshape(1, d)] + args
        body = _oproj_ffn_kernel
    return pl.pallas_call(
        body,
        grid=(n_rows // rows,),
        in_specs=in_specs,
        out_specs=row_spec,
        out_shape=jax.ShapeDtypeStruct((n_rows, d), F32),
        scratch_shapes=[
            pltpu.VMEM((rows, d), BF16),
            pltpu.VMEM((HALO + rows, 2 * c), F32),
            pltpu.VMEM((D_FF // c, HALO, 2 * c), F32),
            pltpu.VMEM((rows, D_FF), BF16),
        ],
        compiler_params=pltpu.CompilerParams(
            dimension_semantics=("arbitrary",), vmem_limit_bytes=VMEM_LIMIT),
        name="conv_glu_ffn",
    )(*args)


def _mixer_kernel(x_ref, gpre_ref, gpost_ref, win_ref, cw_ref, wout_ref, o_ref,
                  h_sc, u_sc, halo_sc, y_sc):
    rows = x_ref.shape[0]
    c = MIX_CHUNK
    first = (pl.program_id(0) % (SEQ // rows)) == 0
    x = x_ref[...]
    h_sc[...] = _rms(x, gpre_ref[...], NORM_EPS).astype(BF16)
    for j in range(D_MODEL // c):
        cs = slice(c * j, c * (j + 1))
        h = h_sc[...]
        b_gate, c_gate, hv = (
            jnp.dot(h, win_ref[:, part * D_MODEL + c * j:part * D_MODEL + c * (j + 1)],
                    preferred_element_type=F32) for part in range(3))
        y = _causal_conv3(u_sc, halo_sc.at[j], c_gate * hv, cw_ref[:, cs], first, rows)
        y_sc[:, cs] = (b_gate * y).astype(BF16)
    m = jnp.dot(y_sc[...], wout_ref[...], preferred_element_type=F32)
    o_ref[...] = x + _rms(m, gpost_ref[...], NORM_EPS)


def _mixer(x2, g_pre, g_post, w_in, conv_w, w_out):
    n_rows, d = x2.shape
    c = MIX_CHUNK
    return pl.pallas_call(
        _mixer_kernel,
        grid=(n_rows // ROW_TILE,),
        in_specs=[
            pl.BlockSpec((ROW_TILE, d), lambda i: (i, 0)),
            _resident((1, d)),
            _resident((1, d)),
            _resident((d, 3 * d)),
            _resident((CONV_WIDTH, d)),
            _resident((d, d)),
        ],
        out_specs=pl.BlockSpec((ROW_TILE, d), lambda i: (i, 0)),
        out_shape=jax.ShapeDtypeStruct((n_rows, d), F32),
        scratch_shapes=[
            pltpu.VMEM((ROW_TILE, d), BF16),
            pltpu.VMEM((HALO + ROW_TILE, c), F32),
            pltpu.VMEM((d // c, HALO, c), F32),
            pltpu.VMEM((ROW_TILE, d), BF16),
        ],
        compiler_params=pltpu.CompilerParams(
            dimension_semantics=("arbitrary",), vmem_limit_bytes=VMEM_LIMIT),
        name="conv_mixer",
    )(x2, g_pre.reshape(1, d), g_post.reshape(1, d), w_in, conv_w, w_out)


def _qkv_layout(w):
    d = w.shape[0]
    qk = N_HEADS * QK_DIM
    q1, q2, k1, k2 = (w[:, i * qk:(i + 1) * qk].reshape(d, N_HEADS, QK_DIM) for i in range(4))
    w_k = jnp.concatenate([k1, k2], axis=-1).reshape(d, 2 * qk).astype(BF16)
    w_q = jnp.concatenate([q1, q2], axis=-1).reshape(d, 2 * qk)
    w_qvt = jnp.concatenate([w_q, w[:, 4 * qk:]], axis=1).T.astype(BF16)
    return w_k, w_qvt


def kernel(x, norm_g, attn_w_qkv, attn_w_o, attn_lambda_q1, attn_lambda_k1, attn_lambda_q2,
           attn_lambda_k2, attn_subln_g, conv_w_in, conv_w, conv_w_out, ffn_w_up, ffn_conv_w,
           ffn_w_down):
    b, s, d = x.shape
    assert (s, d) == (SEQ, D_MODEL) and norm_g.shape == (2, 4, d)
    assert attn_w_qkv.shape[0] == 1 and conv_w_in.shape[0] == 1 and ffn_w_up.shape[0] == 2
    slopes = jnp.exp2(-8.0 * jnp.arange(1, N_HEADS + 1, dtype=F32) / N_HEADS)
    x2 = x.reshape(b * s, d)

    g = norm_g[0]
    lambda_init = 0.8 - 0.6 * math.exp(-0.3 * 0)
    w_k, w_qvt = _qkv_layout(attn_w_qkv[0])
    k, qvt, w_o, w_up0, w_dn0 = _qkv_proj(
        x2.reshape(b, s, d), g[0], w_k, w_qvt,
        cast=[(attn_w_o, 0), (ffn_w_up, 0), (ffn_w_down, 0)])
    o, w_in, w_out, w_up1, w_dn1 = _attention(
        k, qvt, slopes, attn_lambda_q1[0], attn_lambda_k1[0], attn_lambda_q2[0],
        attn_lambda_k2[0], attn_subln_g[0], lambda_init,
        cast=[(conv_w_in, 0), (conv_w_out, 0), (ffn_w_up, 1), (ffn_w_down, 1)])
    x2 = _ffn(x2, g[2], g[3], w_up0, ffn_conv_w[0], w_dn0,
              attn=(o.reshape(b * s, d), w_o, g[1]))

    g = norm_g[1]
    x2 = _mixer(x2, g[0], g[1], w_in, conv_w[0], w_out)
    x2 = _ffn(x2, g[2], g[3], w_up1, ffn_conv_w[1], w_dn1)
    return x2.reshape(b, s, d)
```

```python
import functools
import math

import jax
import jax.numpy as jnp
from jax import lax
from jax.experimental import pallas as pl
from jax.experimental.pallas import tpu as pltpu

D_MODEL = 1024
SEQ = 2048
CHUNK = 64
N_HEADS = 8
QK_DIM = 64
V_DIM = 128
CONV_WIDTH = 3
D_FF = 2816
NORM_EPS = 1e-6
SUBLN_EPS = 1e-5

F32 = jnp.float32
BF16 = jnp.bfloat16

ROW_TILE = 512
ATT_TILE = 256
ATT_HEADS = 2
SCORE_LEAD = 4
S_SLOTS = SCORE_LEAD + 1
BF16_SUBLANES = 16
ONES_ROWS = BF16_SUBLANES
FFN_CHUNK = 256
MIX_CHUNK = 256
HALO = 8
VMEM_LIMIT = 56 * 1024 * 1024
MASK_VALUE = -1e30
LOG2E = math.log2(math.e)

_NT = (((1,), (1,)), ((), ()))


def _rms(x, g, eps):
    ms = jnp.mean(x * x, axis=-1, keepdims=True)
    return x * lax.rsqrt(ms + eps) * g


def _resident(shape):
    nd = len(shape)
    return pl.BlockSpec(shape, lambda *_: (0,) * nd, pipeline_mode=pl.Buffered(1))


def _cast_job(w, layer, n_steps, step_of):
    rows, cols = w.shape[-2:]
    n_blocks = n_steps
    while rows % n_blocks or (rows // n_blocks) % BF16_SUBLANES:
        n_blocks //= 2
    every = n_steps // n_blocks
    rb = rows // n_blocks
    in_spec = pl.BlockSpec((None, rb, cols), lambda *g: (layer, step_of(*g) // every, 0))
    out_spec = pl.BlockSpec((rb, cols), lambda *g: (step_of(*g) // every, 0))
    return in_spec, out_spec, jax.ShapeDtypeStruct((rows, cols), BF16)


def _run_cast_jobs(srcs, dsts):
    for src, dst in zip(srcs, dsts):
        dst[...] = src[...].astype(BF16)


def _qkv_proj_kernel(x_ref, g_ref, wk_ref, wqvt_ref, *refs, n_cast):
    cast_srcs, (k_ref, qvt_ref), cast_dsts = refs[:n_cast], refs[n_cast:n_cast + 2], refs[n_cast + 2:]
    h = _rms(x_ref[...], g_ref[...], NORM_EPS).astype(BF16)
    k_ref[...] = jnp.dot(h, wk_ref[...], preferred_element_type=F32).astype(BF16)
    qvt = lax.dot_general(wqvt_ref[...], h, _NT, preferred_element_type=F32)
    n_q = N_HEADS * 2 * QK_DIM
    qvt_ref[0:n_q, :] = (qvt[0:n_q] * (LOG2E * QK_DIM ** -0.5)).astype(BF16)
    qvt_ref[n_q:, :] = qvt[n_q:].astype(BF16)
    _run_cast_jobs(cast_srcs, cast_dsts)


def _qkv_proj(x3, g, w_k, w_qvt, cast):
    b, s, d = x3.shape
    tiles = s // ROW_TILE
    n_k, n_qv = w_k.shape[1], w_qvt.shape[0]
    jobs = [_cast_job(w, layer, b * tiles, lambda bi, i: bi * tiles + i) for w, layer in cast]
    return pl.pallas_call(
        functools.partial(_qkv_proj_kernel, n_cast=len(jobs)),
        grid=(b, tiles),
        in_specs=[
            pl.BlockSpec((None, ROW_TILE, d), lambda bi, i: (bi, i, 0)),
            _resident((1, d)),
            _resident((d, n_k)),
            _resident((n_qv, d)),
        ] + [job[0] for job in jobs],
        out_specs=[
            pl.BlockSpec((None, ROW_TILE, n_k), lambda bi, i: (bi, i, 0)),
            pl.BlockSpec((None, n_qv, ROW_TILE), lambda bi, i: (bi, 0, i)),
        ] + [job[1] for job in jobs],
        out_shape=[
            jax.ShapeDtypeStruct((b, s, n_k), BF16),
            jax.ShapeDtypeStruct((b, n_qv, s), BF16),
        ] + [job[2] for job in jobs],
        compiler_params=pltpu.CompilerParams(
            dimension_semantics=("arbitrary", "arbitrary"), vmem_limit_bytes=VMEM_LIMIT),
        name="qkv_proj",
    )(x3, g.reshape(1, d), w_k, w_qvt, *[w for w, _ in cast])


def _attn_kernel(slope_ref, lq1_ref, lk1_ref, lq2_ref, lk2_ref, g_ref, kb_ref, db_ref, k_ref,
                 qt_ref, vt_ref, *refs, lambda_init, n_cast):
    cast_srcs, o_ref, cast_dsts = refs[:n_cast], refs[n_cast], refs[n_cast + 1:2 * n_cast + 1]
    scratch = refs[2 * n_cast + 1:]
    lam = (jnp.exp(jnp.sum(lq1_ref[...] * lk1_ref[...], axis=-1, keepdims=True))
           - jnp.exp(jnp.sum(lq2_ref[...] * lk2_ref[...], axis=-1, keepdims=True))
           + lambda_init)
    for hd in range(ATT_HEADS):
        _attn_head(slope_ref[pl.program_id(1) * ATT_HEADS + hd], lam, g_ref,
                   kb_ref.at[hd], db_ref.at[hd],
                   k_ref.at[:, hd * 2 * QK_DIM:(hd + 1) * 2 * QK_DIM],
                   qt_ref.at[hd * 2 * QK_DIM:(hd + 1) * 2 * QK_DIM, :],
                   vt_ref.at[hd * V_DIM:(hd + 1) * V_DIM, :],
                   o_ref.at[:, hd * V_DIM:(hd + 1) * V_DIM],
                   *[sc.at[hd] for sc in scratch], lambda_init=lambda_init)
    _run_cast_jobs(cast_srcs, cast_dsts)


def _attn_head(slope, lam, g_ref, kb_sc, db_sc, k_ref, qt_ref, vt_ref, o_ref, qs_sc, acc_sc,
               s_sc, *, lambda_init):
    t = ATT_TILE
    nq = SEQ // t
    slope = slope * LOG2E
    row = lax.broadcasted_iota(jnp.int32, (2 * QK_DIM, t), 0)

    for qi in range(nq):
        qt = qt_ref[:, qi * t:(qi + 1) * t]
        zero = jnp.zeros_like(qt)
        qs_sc[qi, 0] = jnp.where(row < QK_DIM, qt, zero)
        qs_sc[qi, 1] = jnp.where(row >= QK_DIM, qt, zero)

    tiles = [(kj, qi, mp) for kj in range(nq) for qi in range(kj, nq) for mp in range(2)]

    def emit_scores(idx):
        kj, qi, mp = tiles[idx]
        k_tile = k_ref[kj * t:(kj + 1) * t, :]
        bias = db_sc[...] if qi == kj else kb_sc[...]
        s = jnp.dot(k_tile, qs_sc[qi, mp], preferred_element_type=F32) + bias
        s_sc[idx % S_SLOTS] = s
        s_max[idx] = jnp.max(s, axis=0, keepdims=True)

    s_max = [None] * len(tiles)
    for idx in range(min(SCORE_LEAD, len(tiles))):
        emit_scores(idx)

    ones_rows = jnp.ones((ONES_ROWS, t), BF16)
    m = [[None, None] for _ in range(nq)]
    for idx, (kj, qi, mp) in enumerate(tiles):
        if idx + SCORE_LEAD < len(tiles):
            emit_scores(idx + SCORE_LEAD)
        s = s_sc[idx % S_SLOTS]
        v_ext = jnp.concatenate([vt_ref[:, kj * t:(kj + 1) * t], ones_rows], axis=0)
        shift = slope * jnp.full((1, t), float((qi - kj) * t), F32)
        m_tile = s_max[idx] - shift
        if kj == 0:
            m_new = m_tile
            p = jnp.exp2((s - (m_new + shift)).astype(BF16))
            acc_sc[qi, mp] = jnp.dot(v_ext, p, preferred_element_type=F32)
        else:
            m_new = jnp.maximum(m[qi][mp], m_tile)
            alpha = jnp.exp2(m[qi][mp] - m_new)
            p = jnp.exp2((s - (m_new + shift)).astype(BF16))
            acc_sc[qi, mp] = alpha * acc_sc[qi, mp] + jnp.dot(
                v_ext, p, preferred_element_type=F32)
        m[qi][mp] = m_new
        if qi == kj and mp == 1:
            acc1 = acc_sc[qi, 0]
            acc2 = acc_sc[qi, 1]
            o_t = (acc1[0:V_DIM] / acc1[V_DIM:V_DIM + 1]
                   - lam * (acc2[0:V_DIM] / acc2[V_DIM:V_DIM + 1]))
            ms = jnp.mean(o_t * o_t, axis=0, keepdims=True)
            o = (o_t * lax.rsqrt(ms + SUBLN_EPS)).T
            o = o * g_ref[...] * (1.0 - lambda_init)
            o_ref[qi * t:(qi + 1) * t, :] = o.astype(o_ref.dtype)


def _attention(k, qvt, slopes, lq1, lk1, lq2, lk2, subln_g, lambda_init, cast):
    b = k.shape[0]
    t = ATT_TILE
    smem = pl.BlockSpec(memory_space=pltpu.SMEM)
    lam_spec = _resident((1, QK_DIM))
    hd = ATT_HEADS
    steps = N_HEADS // hd
    jobs = [_cast_job(w, layer, b * steps, lambda bi, h: bi * steps + h) for w, layer in cast]
    key = jnp.arange(t, dtype=F32)[:, None]
    qry = jnp.arange(t, dtype=F32)[None, :]
    slope2 = (slopes * LOG2E)[:, None, None]
    kb = slope2 * jnp.broadcast_to(key, (t, t))
    db = jnp.where((key // CHUNK) <= (qry // CHUNK), slope2 * (qry - jnp.abs(qry - key)),
                   MASK_VALUE)
    table_spec = pl.BlockSpec((hd, t, t), lambda bi, h: (h, 0, 0))
    return pl.pallas_call(
        functools.partial(_attn_kernel, lambda_init=lambda_init, n_cast=len(jobs)),
        grid=(b, steps),
        in_specs=[
            smem, lam_spec, lam_spec, lam_spec, lam_spec, _resident((1, V_DIM)),
            table_spec, table_spec,
            pl.BlockSpec((None, SEQ, hd * 2 * QK_DIM), lambda bi, h: (bi, 0, h)),
            pl.BlockSpec((None, hd * 2 * QK_DIM, SEQ), lambda bi, h: (bi, h, 0)),
            pl.BlockSpec((None, hd * V_DIM, SEQ), lambda bi, h: (bi, steps + h, 0)),
        ] + [job[0] for job in jobs],
        out_specs=[pl.BlockSpec((None, SEQ, hd * V_DIM), lambda bi, h: (bi, 0, h))]
        + [job[1] for job in jobs],
        out_shape=[jax.ShapeDtypeStruct((b, SEQ, N_HEADS * V_DIM), BF16)]
        + [job[2] for job in jobs],
        scratch_shapes=[
            pltpu.VMEM((hd, SEQ // t, 2, 2 * QK_DIM, t), BF16),
            pltpu.VMEM((hd, SEQ // t, 2, V_DIM + ONES_ROWS, t), F32),
            pltpu.VMEM((hd, S_SLOTS, t, t), F32),
        ],
        compiler_params=pltpu.CompilerParams(
            dimension_semantics=("arbitrary", "arbitrary"), vmem_limit_bytes=VMEM_LIMIT),
        name="diff_attention",
    )(slopes, lq1.reshape(1, QK_DIM), lk1.reshape(1, QK_DIM), lq2.reshape(1, QK_DIM),
      lk2.reshape(1, QK_DIM), subln_g.reshape(1, V_DIM), kb, db, k, qvt, qvt,
      *[w for w, _ in cast])


def _causal_conv3(u_sc, halo_ref, u, cw, first, rows):
    hist = halo_ref[...]
    u_sc[0:HALO, :] = jnp.where(first, jnp.zeros_like(hist), hist)
    u_sc[HALO:HALO + rows, :] = u
    halo_ref[...] = u[rows - HALO:rows, :]
    return (u * cw[2:3, :]
            + u_sc[HALO - 1:HALO - 1 + rows, :] * cw[1:2, :]
            + u_sc[HALO - 2:HALO - 2 + rows, :] * cw[0:1, :])


def _ffn_tile(x, first, gpre_ref, gpost_ref, wup_ref, cw_ref, wdn_ref,
              h_sc, u_sc, halo_sc, act_sc):
    rows = x.shape[0]
    c = FFN_CHUNK
    h_sc[...] = _rms(x, gpre_ref[...], NORM_EPS).astype(BF16)
    for j in range(D_FF // c):
        gs = slice(c * j, c * (j + 1))
        vs = slice(D_FF + c * j, D_FF + c * (j + 1))
        h = h_sc[...]
        u = jnp.concatenate(
            [jnp.dot(h, wup_ref[:, gs], preferred_element_type=F32),
             jnp.dot(h, wup_ref[:, vs], preferred_element_type=F32)], axis=1)
        cw = jnp.concatenate([cw_ref[:, gs], cw_ref[:, vs]], axis=1)
        y = _causal_conv3(u_sc, halo_sc.at[j], u, cw, first, rows)
        gate = y[:, 0:c]
        act_sc[:, gs] = (gate * jax.nn.sigmoid(gate) * y[:, c:2 * c]).astype(BF16)
    m = jnp.dot(act_sc[...], wdn_ref[...], preferred_element_type=F32)
    return x + _rms(m, gpost_ref[...], NORM_EPS)


def _ffn_kernel(x_ref, gpre_ref, gpost_ref, wup_ref, cw_ref, wdn_ref, o_ref, *scratch):
    first = (pl.program_id(0) % (SEQ // x_ref.shape[0])) == 0
    o_ref[...] = _ffn_tile(x_ref[...], first, gpre_ref, gpost_ref, wup_ref, cw_ref, wdn_ref,
                           *scratch)


def _oproj_ffn_kernel(a_ref, wo_ref, gmix_ref, x_ref, gpre_ref, gpost_ref, wup_ref, cw_ref,
                      wdn_ref, o_ref, *scratch):
    first = (pl.program_id(0) % (SEQ // x_ref.shape[0])) == 0
    mix = jnp.dot(a_ref[...], wo_ref[...], preferred_element_type=F32)
    x = x_ref[...] + _rms(mix, gmix_ref[...], NORM_EPS)
    o_ref[...] = _ffn_tile(x, first, gpre_ref, gpost_ref, wup_ref, cw_ref, wdn_ref, *scratch)


def _ffn(x2, g_pre, g_post, w_up, conv_w, w_down, attn=None):
    n_rows, d = x2.shape
    c = FFN_CHUNK
    row_spec = pl.BlockSpec((ROW_TILE, d), lambda i: (i, 0))
    in_specs = [
        row_spec,
        _resident((1, d)),
        _resident((1, d)),
        _resident((d, 2 * D_FF)),
        _resident((CONV_WIDTH, 2 * D_FF)),
        _resident((D_FF, d)),
    ]
    args = [x2, g_pre.reshape(1, d), g_post.reshape(1, d), w_up, conv_w, w_down]
    body = _ffn_kernel
    if attn is not None:
        a2, w_o, g_mix = attn
        in_specs = [pl.BlockSpec((ROW_TILE, a2.shape[1]), lambda i: (i, 0)),
                    _resident(w_o.shape), _resident((1, d))] + in_specs
        args = [a2, w_o, g_mix.reshape(1, d)] + args
        body = _oproj_ffn_kernel
    return pl.pallas_call(
        body,
        grid=(n_rows // ROW_TILE,),
        in_specs=in_specs,
        out_specs=row_spec,
        out_shape=jax.ShapeDtypeStruct((n_rows, d), F32),
        scratch_shapes=[
            pltpu.VMEM((ROW_TILE, d), BF16),
            pltpu.VMEM((HALO + ROW_TILE, 2 * c), F32),
            pltpu.VMEM((D_FF // c, HALO, 2 * c), F32),
            pltpu.VMEM((ROW_TILE, D_FF), BF16),
        ],
        compiler_params=pltpu.CompilerParams(
            dimension_semantics=("arbitrary",), vmem_limit_bytes=VMEM_LIMIT),
        name="conv_glu_ffn",
    )(*args)


def _mixer_kernel(x_ref, gpre_ref, gpost_ref, win_ref, cw_ref, wout_ref, o_ref,
                  h_sc, u_sc, halo_sc, y_sc):
    rows = x_ref.shape[0]
    c = MIX_CHUNK
    first = (pl.program_id(0) % (SEQ // rows)) == 0
    x = x_ref[...]
    h_sc[...] = _rms(x, gpre_ref[...], NORM_EPS).astype(BF16)
    for j in range(D_MODEL // c):
        cs = slice(c * j, c * (j + 1))
        h = h_sc[...]
        b_gate, c_gate, hv = (
            jnp.dot(h, win_ref[:, part * D_MODEL + c * j:part * D_MODEL + c * (j + 1)],
                    preferred_element_type=F32) for part in range(3))
        y = _causal_conv3(u_sc, halo_sc.at[j], c_gate * hv, cw_ref[:, cs], first, rows)
        y_sc[:, cs] = (b_gate * y).astype(BF16)
    m = jnp.dot(y_sc[...], wout_ref[...], preferred_element_type=F32)
    o_ref[...] = x + _rms(m, gpost_ref[...], NORM_EPS)


def _mixer(x2, g_pre, g_post, w_in, conv_w, w_out):
    n_rows, d = x2.shape
    c = MIX_CHUNK
    return pl.pallas_call(
        _mixer_kernel,
        grid=(n_rows // ROW_TILE,),
        in_specs=[
            pl.BlockSpec((ROW_TILE, d), lambda i: (i, 0)),
            _resident((1, d)),
            _resident((1, d)),
            _resident((d, 3 * d)),
            _resident((CONV_WIDTH, d)),
            _resident((d, d)),
        ],
        out_specs=pl.BlockSpec((ROW_TILE, d), lambda i: (i, 0)),
        out_shape=jax.ShapeDtypeStruct((n_rows, d), F32),
        scratch_shapes=[
            pltpu.VMEM((ROW_TILE, d), BF16),
            pltpu.VMEM((HALO + ROW_TILE, c), F32),
            pltpu.VMEM((d // c, HALO, c), F32),
            pltpu.VMEM((ROW_TILE, d), BF16),
        ],
        compiler_params=pltpu.CompilerParams(
            dimension_semantics=("arbitrary",), vmem_limit_bytes=VMEM_LIMIT),
        name="conv_mixer",
    )(x2, g_pre.reshape(1, d), g_post.reshape(1, d), w_in, conv_w, w_out)


def _qkv_layout(w):
    d = w.shape[0]
    qk = N_HEADS * QK_DIM
    q1, q2, k1, k2 = (w[:, i * qk:(i + 1) * qk].reshape(d, N_HEADS, QK_DIM) for i in range(4))
    w_k = jnp.concatenate([k1, k2], axis=-1).reshape(d, 2 * qk).astype(BF16)
    w_q = jnp.concatenate([q1, q2], axis=-1).reshape(d, 2 * qk)
    w_qvt = jnp.concatenate([w_q, w[:, 4 * qk:]], axis=1).astype(BF16).T
    return w_k, w_qvt


def kernel(x, norm_g, attn_w_qkv, attn_w_o, attn_lambda_q1, attn_lambda_k1, attn_lambda_q2,
           attn_lambda_k2, attn_subln_g, conv_w_in, conv_w, conv_w_out, ffn_w_up, ffn_conv_w,
           ffn_w_down):
    b, s, d = x.shape
    assert (s, d) == (SEQ, D_MODEL) and norm_g.shape == (2, 4, d)
    assert attn_w_qkv.shape[0] == 1 and conv_w_in.shape[0] == 1 and ffn_w_up.shape[0] == 2
    slopes = jnp.exp2(-8.0 * jnp.arange(1, N_HEADS + 1, dtype=F32) / N_HEADS)
    x2 = x.reshape(b * s, d)

    g = norm_g[0]
    lambda_init = 0.8 - 0.6 * math.exp(-0.3 * 0)
    w_k, w_qvt = _qkv_layout(attn_w_qkv[0])
    k, qvt, w_o, w_up0, w_dn0 = _qkv_proj(
        x2.reshape(b, s, d), g[0], w_k, w_qvt,
        cast=[(attn_w_o, 0), (ffn_w_up, 0), (ffn_w_down, 0)])
    o, w_in, w_out, w_up1, w_dn1 = _attention(
        k, qvt, slopes, attn_lambda_q1[0], attn_lambda_k1[0], attn_lambda_q2[0],
        attn_lambda_k2[0], attn_subln_g[0], lambda_init,
        cast=[(conv_w_in, 0), (conv_w_out, 0), (ffn_w_up, 1), (ffn_w_down, 1)])
    x2 = _ffn(x2, g[2], g[3], w_up0, ffn_conv_w[0], w_dn0,
              attn=(o.reshape(b * s, d), w_o, g[1]))

    g = norm_g[1]
    x2 = _mixer(x2, g[0], g[1], w_in, conv_w[0], w_out)
    x2 = _ffn(x2, g[2], g[3], w_up1, ffn_conv_w[1], w_dn1)
    return x2.reshape(b, s, d)
```

```python
import functools
import math

import jax
import jax.numpy as jnp
from jax import lax
from jax.experimental import pallas as pl
from jax.experimental.pallas import tpu as pltpu

D_MODEL = 1024
SEQ = 2048
CHUNK = 64
N_HEADS = 8
QK_DIM = 64
V_DIM = 128
CONV_WIDTH = 3
D_FF = 2816
NORM_EPS = 1e-6
SUBLN_EPS = 1e-5

F32 = jnp.float32
BF16 = jnp.bfloat16

ROW_TILE = 512
ATT_TILE = 256
ATT_HEADS = 2
SCORE_LEAD = 4
S_SLOTS = SCORE_LEAD + 1
BF16_SUBLANES = 16
ONES_ROWS = BF16_SUBLANES
FFN_CHUNK = 256
MIX_CHUNK = 256
HALO = 8
LANES = 128
CONV_PHASES = 4
VMEM_LIMIT = 56 * 1024 * 1024
MASK_VALUE = -1e30
LOG2E = math.log2(math.e)

_NT = (((1,), (1,)), ((), ()))


def _rms(x, g, eps):
    ms = jnp.mean(x * x, axis=-1, keepdims=True)
    return x * lax.rsqrt(ms + eps) * g


def _resident(shape):
    nd = len(shape)
    return pl.BlockSpec(shape, lambda *_: (0,) * nd, pipeline_mode=pl.Buffered(1))


def _cast_job(w, layer, n_steps, step_of):
    rows, cols = w.shape[-2:]
    n_blocks = n_steps
    while rows % n_blocks or (rows // n_blocks) % BF16_SUBLANES:
        n_blocks //= 2
    every = n_steps // n_blocks
    rb = rows // n_blocks
    in_spec = pl.BlockSpec((None, rb, cols), lambda *g: (layer, step_of(*g) // every, 0))
    out_spec = pl.BlockSpec((rb, cols), lambda *g: (step_of(*g) // every, 0))
    return in_spec, out_spec, jax.ShapeDtypeStruct((rows, cols), BF16)


def _run_cast_jobs(srcs, dsts):
    for src, dst in zip(srcs, dsts):
        dst[...] = src[...].astype(BF16)


def _qkv_proj_kernel(x_ref, g_ref, wk_ref, wqvt_ref, *refs, n_cast):
    cast_srcs, (k_ref, qvt_ref), cast_dsts = refs[:n_cast], refs[n_cast:n_cast + 2], refs[n_cast + 2:]
    h = _rms(x_ref[...], g_ref[...], NORM_EPS).astype(BF16)
    k_ref[...] = jnp.dot(h, wk_ref[...], preferred_element_type=F32).astype(BF16)
    qvt = lax.dot_general(wqvt_ref[...], h, _NT, preferred_element_type=F32)
    n_q = N_HEADS * 2 * QK_DIM
    qvt_ref[0:n_q, :] = (qvt[0:n_q] * (LOG2E * QK_DIM ** -0.5)).astype(BF16)
    qvt_ref[n_q:, :] = qvt[n_q:].astype(BF16)
    _run_cast_jobs(cast_srcs, cast_dsts)


def _qkv_proj(x3, g, w_k, w_qvt, cast):
    b, s, d = x3.shape
    tiles = s // ROW_TILE
    n_k, n_qv = w_k.shape[1], w_qvt.shape[0]
    jobs = [_cast_job(w, layer, b * tiles, lambda bi, i: bi * tiles + i) for w, layer in cast]
    return pl.pallas_call(
        functools.partial(_qkv_proj_kernel, n_cast=len(jobs)),
        grid=(b, tiles),
        in_specs=[
            pl.BlockSpec((None, ROW_TILE, d), lambda bi, i: (bi, i, 0)),
            _resident((1, d)),
            _resident((d, n_k)),
            _resident((n_qv, d)),
        ] + [job[0] for job in jobs],
        out_specs=[
            pl.BlockSpec((None, ROW_TILE, n_k), lambda bi, i: (bi, i, 0)),
            pl.BlockSpec((None, n_qv, ROW_TILE), lambda bi, i: (bi, 0, i)),
        ] + [job[1] for job in jobs],
        out_shape=[
            jax.ShapeDtypeStruct((b, s, n_k), BF16),
            jax.ShapeDtypeStruct((b, n_qv, s), BF16),
        ] + [job[2] for job in jobs],
        compiler_params=pltpu.CompilerParams(
            dimension_semantics=("arbitrary", "arbitrary"), vmem_limit_bytes=VMEM_LIMIT),
        name="qkv_proj",
    )(x3, g.reshape(1, d), w_k, w_qvt, *[w for w, _ in cast])


def _attn_kernel(slope_ref, lq1_ref, lk1_ref, lq2_ref, lk2_ref, g_ref, kb_ref, db_ref, k_ref,
                 qt_ref, vt_ref, *refs, lambda_init, n_cast):
    cast_srcs, o_ref, cast_dsts = refs[:n_cast], refs[n_cast], refs[n_cast + 1:2 * n_cast + 1]
    scratch = refs[2 * n_cast + 1:]
    lam = (jnp.exp(jnp.sum(lq1_ref[...] * lk1_ref[...], axis=-1, keepdims=True))
           - jnp.exp(jnp.sum(lq2_ref[...] * lk2_ref[...], axis=-1, keepdims=True))
           + lambda_init)
    for hd in range(ATT_HEADS):
        _attn_head(slope_ref[pl.program_id(1) * ATT_HEADS + hd], lam, g_ref,
                   kb_ref.at[hd], db_ref.at[hd],
                   k_ref.at[:, hd * 2 * QK_DIM:(hd + 1) * 2 * QK_DIM],
                   qt_ref.at[hd * 2 * QK_DIM:(hd + 1) * 2 * QK_DIM, :],
                   vt_ref.at[hd * V_DIM:(hd + 1) * V_DIM, :],
                   o_ref.at[:, hd * V_DIM:(hd + 1) * V_DIM],
                   *[sc.at[hd] for sc in scratch], lambda_init=lambda_init)
    _run_cast_jobs(cast_srcs, cast_dsts)


def _attn_head(slope, lam, g_ref, kb_sc, db_sc, k_ref, qt_ref, vt_ref, o_ref, qs_sc, acc_sc,
               s_sc, *, lambda_init):
    t = ATT_TILE
    nq = SEQ // t
    slope = slope * LOG2E
    row = lax.broadcasted_iota(jnp.int32, (2 * QK_DIM, t), 0)

    for qi in range(nq):
        qt = qt_ref[:, qi * t:(qi + 1) * t]
        zero = jnp.zeros_like(qt)
        qs_sc[qi, 0] = jnp.where(row < QK_DIM, qt, zero)
        qs_sc[qi, 1] = jnp.where(row >= QK_DIM, qt, zero)

    tiles = [(kj, qi, mp) for kj in range(nq) for qi in range(kj, nq) for mp in range(2)]

    def emit_scores(idx):
        kj, qi, mp = tiles[idx]
        k_tile = k_ref[kj * t:(kj + 1) * t, :]
        bias = db_sc[...] if qi == kj else kb_sc[...]
        s = jnp.dot(k_tile, qs_sc[qi, mp], preferred_element_type=F32) + bias
        s_sc[idx % S_SLOTS] = s
        s_max[idx] = jnp.max(s, axis=0, keepdims=True)

    s_max = [None] * len(tiles)
    for idx in range(min(SCORE_LEAD, len(tiles))):
        emit_scores(idx)

    ones_rows = jnp.ones((ONES_ROWS, t), BF16)
    m = [[None, None] for _ in range(nq)]
    for idx, (kj, qi, mp) in enumerate(tiles):
        if idx + SCORE_LEAD < len(tiles):
            emit_scores(idx + SCORE_LEAD)
        s = s_sc[idx % S_SLOTS]
        v_ext = jnp.concatenate([vt_ref[:, kj * t:(kj + 1) * t], ones_rows], axis=0)
        shift = slope * jnp.full((1, t), float((qi - kj) * t), F32)
        m_tile = s_max[idx] - shift
        if kj == 0:
            m_new = m_tile
            p = jnp.exp2((s - (m_new + shift)).astype(BF16))
            acc_sc[qi, mp] = jnp.dot(v_ext, p, preferred_element_type=F32)
        else:
            m_new = jnp.maximum(m[qi][mp], m_tile)
            alpha = jnp.exp2(m[qi][mp] - m_new)
            p = jnp.exp2((s - (m_new + shift)).astype(BF16))
            acc_sc[qi, mp] = alpha * acc_sc[qi, mp] + jnp.dot(
                v_ext, p, preferred_element_type=F32)
        m[qi][mp] = m_new
        if qi == kj and mp == 1:
            acc1 = acc_sc[qi, 0]
            acc2 = acc_sc[qi, 1]
            o_t = (acc1[0:V_DIM] / acc1[V_DIM:V_DIM + 1]
                   - lam * (acc2[0:V_DIM] / acc2[V_DIM:V_DIM + 1]))
            ms = jnp.mean(o_t * o_t, axis=0, keepdims=True)
            o = (o_t * lax.rsqrt(ms + SUBLN_EPS)).T
            o = o * g_ref[...] * (1.0 - lambda_init)
            o_ref[qi * t:(qi + 1) * t, :] = o.astype(o_ref.dtype)


def _attention(k, qvt, slopes, lq1, lk1, lq2, lk2, subln_g, lambda_init, cast):
    b = k.shape[0]
    t = ATT_TILE
    smem = pl.BlockSpec(memory_space=pltpu.SMEM)
    lam_spec = _resident((1, QK_DIM))
    hd = ATT_HEADS
    steps = N_HEADS // hd
    jobs = [_cast_job(w, layer, b * steps, lambda bi, h: bi * steps + h) for w, layer in cast]
    key = jnp.arange(t, dtype=F32)[:, None]
    qry = jnp.arange(t, dtype=F32)[None, :]
    slope2 = (slopes * LOG2E)[:, None, None]
    kb = slope2 * jnp.broadcast_to(key, (t, t))
    db = jnp.where((key // CHUNK) <= (qry // CHUNK), slope2 * (qry - jnp.abs(qry - key)),
                   MASK_VALUE)
    table_spec = pl.BlockSpec((hd, t, t), lambda bi, h: (h, 0, 0))
    return pl.pallas_call(
        functools.partial(_attn_kernel, lambda_init=lambda_init, n_cast=len(jobs)),
        grid=(b, steps),
        in_specs=[
            smem, lam_spec, lam_spec, lam_spec, lam_spec, _resident((1, V_DIM)),
            table_spec, table_spec,
            pl.BlockSpec((None, SEQ, hd * 2 * QK_DIM), lambda bi, h: (bi, 0, h)),
            pl.BlockSpec((None, hd * 2 * QK_DIM, SEQ), lambda bi, h: (bi, h, 0)),
            pl.BlockSpec((None, hd * V_DIM, SEQ), lambda bi, h: (bi, steps + h, 0)),
        ] + [job[0] for job in jobs],
        out_specs=[pl.BlockSpec((None, SEQ, hd * V_DIM), lambda bi, h: (bi, 0, h))]
        + [job[1] for job in jobs],
        out_shape=[jax.ShapeDtypeStruct((b, SEQ, N_HEADS * V_DIM), BF16)]
        + [job[2] for job in jobs],
        scratch_shapes=[
            pltpu.VMEM((hd, SEQ // t, 2, 2 * QK_DIM, t), BF16),
            pltpu.VMEM((hd, SEQ // t, 2, V_DIM + ONES_ROWS, t), F32),
            pltpu.VMEM((hd, S_SLOTS, t, t), F32),
        ],
        compiler_params=pltpu.CompilerParams(
            dimension_semantics=("arbitrary", "arbitrary"), vmem_limit_bytes=VMEM_LIMIT),
        name="diff_attention",
    )(slopes, lq1.reshape(1, QK_DIM), lk1.reshape(1, QK_DIM), lq2.reshape(1, QK_DIM),
      lk2.reshape(1, QK_DIM), subln_g.reshape(1, V_DIM), kb, db, k, qvt, qvt,
      *[w for w, _ in cast])


def _causal_conv3(u_sc, halo_ref, u, cw, first, rows):
    hist = halo_ref[...]
    hist = jnp.where(first, jnp.zeros_like(hist), hist)
    halo_ref[...] = u[rows - HALO:rows, :]
    out = []
    for slab in range(u.shape[1] // LANES):
        ls = slice(slab * LANES, (slab + 1) * LANES)
        u_sc[slab, 0:HALO, :] = hist[:, ls]
        u_sc[slab, HALO:HALO + rows, :] = u[:, ls]
        phases = []
        for k in range(CONV_PHASES):
            cur, prev1, prev2 = (
                u_sc[slab, pl.ds(HALO + k - back, rows // CONV_PHASES, stride=CONV_PHASES), :]
                for back in range(CONV_WIDTH))
            phases.append(cur * cw[2:3, ls] + prev1 * cw[1:2, ls] + prev2 * cw[0:1, ls])
        out.append(phases)
    return out


def _store_phases(dst_sc, slab, phases):
    for k, val in enumerate(phases):
        dst_sc[slab, pl.ds(k, val.shape[0], stride=CONV_PHASES), :] = val


def _ffn_tile(x, first, gpre_ref, gpost_ref, wup_ref, cw_ref, wdn_ref,
              h_sc, u_sc, halo_sc, a_sc, act_sc):
    rows = x.shape[0]
    c = FFN_CHUNK
    n_slab = c // LANES
    h_sc[...] = _rms(x, gpre_ref[...], NORM_EPS).astype(BF16)
    for j in range(D_FF // c):
        gs = slice(c * j, c * (j + 1))
        vs = slice(D_FF + c * j, D_FF + c * (j + 1))
        h = h_sc[...]
        u = jnp.concatenate(
            [jnp.dot(h, wup_ref[:, gs], preferred_element_type=F32),
             jnp.dot(h, wup_ref[:, vs], preferred_element_type=F32)], axis=1)
        cw = jnp.concatenate([cw_ref[:, gs], cw_ref[:, vs]], axis=1)
        y = _causal_conv3(u_sc, halo_sc.at[j], u, cw, first, rows)
        for slab in range(n_slab):
            _store_phases(a_sc, slab, [gate * jax.nn.sigmoid(gate) * val
                                       for gate, val in zip(y[slab], y[n_slab + slab])])
        act_sc[:, gs] = jnp.concatenate(
            [a_sc[slab] for slab in range(n_slab)], axis=1).astype(BF16)
    m = jnp.dot(act_sc[...], wdn_ref[...], preferred_element_type=F32)
    return x + _rms(m, gpost_ref[...], NORM_EPS)


def _ffn_kernel(x_ref, gpre_ref, gpost_ref, wup_ref, cw_ref, wdn_ref, o_ref, *scratch):
    first = (pl.program_id(0) % (SEQ // x_ref.shape[0])) == 0
    o_ref[...] = _ffn_tile(x_ref[...], first, gpre_ref, gpost_ref, wup_ref, cw_ref, wdn_ref,
                           *scratch)


def _oproj_ffn_kernel(a_ref, wo_ref, gmix_ref, x_ref, gpre_ref, gpost_ref, wup_ref, cw_ref,
                      wdn_ref, o_ref, *scratch):
    first = (pl.program_id(0) % (SEQ // x_ref.shape[0])) == 0
    mix = jnp.dot(a_ref[...], wo_ref[...], preferred_element_type=F32)
    x = x_ref[...] + _rms(mix, gmix_ref[...], NORM_EPS)
    o_ref[...] = _ffn_tile(x, first, gpre_ref, gpost_ref, wup_ref, cw_ref, wdn_ref, *scratch)


def _ffn(x2, g_pre, g_post, w_up, conv_w, w_down, attn=None):
    n_rows, d = x2.shape
    c = FFN_CHUNK
    row_spec = pl.BlockSpec((ROW_TILE, d), lambda i: (i, 0))
    in_specs = [
        row_spec,
        _resident((1, d)),
        _resident((1, d)),
        _resident((d, 2 * D_FF)),
        _resident((CONV_WIDTH, 2 * D_FF)),
        _resident((D_FF, d)),
    ]
    args = [x2, g_pre.reshape(1, d), g_post.reshape(1, d), w_up, conv_w, w_down]
    body = _ffn_kernel
    if attn is not None:
        a2, w_o, g_mix = attn
        in_specs = [pl.BlockSpec((ROW_TILE, a2.shape[1]), lambda i: (i, 0)),
                    _resident(w_o.shape), _resident((1, d))] + in_specs
        args = [a2, w_o, g_mix.reshape(1, d)] + args
        body = _oproj_ffn_kernel
    return pl.pallas_call(
        body,
        grid=(n_rows // ROW_TILE,),
        in_specs=in_specs,
        out_specs=row_spec,
        out_shape=jax.ShapeDtypeStruct((n_rows, d), F32),
        scratch_shapes=[
            pltpu.VMEM((ROW_TILE, d), BF16),
            pltpu.VMEM((2 * c // LANES, HALO + ROW_TILE, LANES), F32),
            pltpu.VMEM((D_FF // c, HALO, 2 * c), F32),
            pltpu.VMEM((c // LANES, ROW_TILE, LANES), F32),
            pltpu.VMEM((ROW_TILE, D_FF), BF16),
        ],
        compiler_params=pltpu.CompilerParams(
            dimension_semantics=("arbitrary",), vmem_limit_bytes=VMEM_LIMIT),
        name="conv_glu_ffn",
    )(*args)


def _mixer_kernel(x_ref, gpre_ref, gpost_ref, win_ref, cw_ref, wout_ref, o_ref,
                  h_sc, u_sc, halo_sc, a_sc, y_sc):
    rows = x_ref.shape[0]
    c = MIX_CHUNK
    n_slab = c // LANES
    first = (pl.program_id(0) % (SEQ // rows)) == 0
    x = x_ref[...]
    h_sc[...] = _rms(x, gpre_ref[...], NORM_EPS).astype(BF16)
    for j in range(D_MODEL // c):
        cs = slice(c * j, c * (j + 1))
        h = h_sc[...]
        b_gate, c_gate, hv = (
            jnp.dot(h, win_ref[:, part * D_MODEL + c * j:part * D_MODEL + c * (j + 1)],
                    preferred_element_type=F32) for part in range(3))
        y = _causal_conv3(u_sc, halo_sc.at[j], c_gate * hv, cw_ref[:, cs], first, rows)
        for slab in range(n_slab):
            _store_phases(a_sc, slab, y[slab])
        conv = jnp.concatenate([a_sc[slab] for slab in range(n_slab)], axis=1)
        y_sc[:, cs] = (b_gate * conv).astype(BF16)
    m = jnp.dot(y_sc[...], wout_ref[...], preferred_element_type=F32)
    o_ref[...] = x + _rms(m, gpost_ref[...], NORM_EPS)


def _mixer(x2, g_pre, g_post, w_in, conv_w, w_out):
    n_rows, d = x2.shape
    c = MIX_CHUNK
    return pl.pallas_call(
        _mixer_kernel,
        grid=(n_rows // ROW_TILE,),
        in_specs=[
            pl.BlockSpec((ROW_TILE, d), lambda i: (i, 0)),
            _resident((1, d)),
            _resident((1, d)),
            _resident((d, 3 * d)),
            _resident((CONV_WIDTH, d)),
            _resident((d, d)),
        ],
        out_specs=pl.BlockSpec((ROW_TILE, d), lambda i: (i, 0)),
        out_shape=jax.ShapeDtypeStruct((n_rows, d), F32),
        scratch_shapes=[
            pltpu.VMEM((ROW_TILE, d), BF16),
            pltpu.VMEM((c // LANES, HALO + ROW_TILE, LANES), F32),
            pltpu.VMEM((d // c, HALO, c), F32),
            pltpu.VMEM((c // LANES, ROW_TILE, LANES), F32),
            pltpu.VMEM((ROW_TILE, d), BF16),
        ],
        compiler_params=pltpu.CompilerParams(
            dimension_semantics=("arbitrary",), vmem_limit_bytes=VMEM_LIMIT),
        name="conv_mixer",
    )(x2, g_pre.reshape(1, d), g_post.reshape(1, d), w_in, conv_w, w_out)


def _qkv_layout(w):
    d = w.shape[0]
    qk = N_HEADS * QK_DIM
    q1, q2, k1, k2 = (w[:, i * qk:(i + 1) * qk].reshape(d, N_HEADS, QK_DIM) for i in range(4))
    w_k = jnp.concatenate([k1, k2], axis=-1).reshape(d, 2 * qk).astype(BF16)
    w_q = jnp.concatenate([q1, q2], axis=-1).reshape(d, 2 * qk)
    w_qvt = jnp.concatenate([w_q, w[:, 4 * qk:]], axis=1).astype(BF16).T
    return w_k, w_qvt


def kernel(x, norm_g, attn_w_qkv, attn_w_o, attn_lambda_q1, attn_lambda_k1, attn_lambda_q2,
           attn_lambda_k2, attn_subln_g, conv_w_in, conv_w, conv_w_out, ffn_w_up, ffn_conv_w,
           ffn_w_down):
    b, s, d = x.shape
    assert (s, d) == (SEQ, D_MODEL) and norm_g.shape == (2, 4, d)
    assert attn_w_qkv.shape[0] == 1 and conv_w_in.shape[0] == 1 and ffn_w_up.shape[0] == 2
    slopes = jnp.exp2(-8.0 * jnp.arange(1, N_HEADS + 1, dtype=F32) / N_HEADS)
    x2 = x.reshape(b * s, d)

    g = norm_g[0]
    lambda_init = 0.8 - 0.6 * math.exp(-0.3 * 0)
    w_k, w_qvt = _qkv_layout(attn_w_qkv[0])
    k, qvt, w_o, w_up0, w_dn0 = _qkv_proj(
        x2.reshape(b, s, d), g[0], w_k, w_qvt,
        cast=[(attn_w_o, 0), (ffn_w_up, 0), (ffn_w_down, 0)])
    o, w_in, w_out, w_up1, w_dn1 = _attention(
        k, qvt, slopes, attn_lambda_q1[0], attn_lambda_k1[0], attn_lambda_q2[0],
        attn_lambda_k2[0], attn_subln_g[0], lambda_init,
        cast=[(conv_w_in, 0), (conv_w_out, 0), (ffn_w_up, 1), (ffn_w_down, 1)])
    x2 = _ffn(x2, g[2], g[3], w_up0, ffn_conv_w[0], w_dn0,
              attn=(o.reshape(b * s, d), w_o, g[1]))

    g = norm_g[1]
    x2 = _mixer(x2, g[0], g[1], w_in, conv_w[0], w_out)
    x2 = _ffn(x2, g[2], g[3], w_up1, ffn_conv_w[1], w_dn1)
    return x2.reshape(b, s, d)
```

```python
import functools
import math

import jax
import jax.numpy as jnp
from jax import lax
from jax.experimental import pallas as pl
from jax.experimental.pallas import tpu as pltpu

D_MODEL = 1024
SEQ = 2048
CHUNK = 64
N_HEADS = 8
QK_DIM = 64
V_DIM = 128
QK_COLS = N_HEADS * 4 * QK_DIM
CONV_WIDTH = 3
D_FF = 2816
NORM_EPS = 1e-6
SUBLN_EPS = 1e-5

F32 = jnp.float32
BF16 = jnp.bfloat16

ROW_TILE = 512
ATT_TILE = 256
ATT_HEADS = 2
SCORE_LEAD = 4
S_SLOTS = SCORE_LEAD + 1
BF16_SUBLANES = 16
ONES_ROWS = BF16_SUBLANES
FFN_CHUNK = 256
MIX_CHUNK = 256
HALO = 8
VMEM_LIMIT = 56 * 1024 * 1024
MASK_VALUE = -1e30
LOG2E = math.log2(math.e)

_NT = (((1,), (1,)), ((), ()))


def _rms(x, g, eps):
    ms = jnp.mean(x * x, axis=-1, keepdims=True)
    return x * lax.rsqrt(ms + eps) * g


def _resident(shape):
    nd = len(shape)
    return pl.BlockSpec(shape, lambda *_: (0,) * nd, pipeline_mode=pl.Buffered(1))


def _cast_job(w, layer, n_steps, step_of):
    rows, cols = w.shape[-2:]
    n_blocks = n_steps
    while rows % n_blocks or (rows // n_blocks) % BF16_SUBLANES:
        n_blocks //= 2
    every = n_steps // n_blocks
    rb = rows // n_blocks
    in_spec = pl.BlockSpec((None, rb, cols), lambda *g: (layer, step_of(*g) // every, 0))
    out_spec = pl.BlockSpec((rb, cols), lambda *g: (step_of(*g) // every, 0))
    return in_spec, out_spec, jax.ShapeDtypeStruct((rows, cols), BF16)


def _run_cast_jobs(srcs, dsts):
    for src, dst in zip(srcs, dsts):
        dst[...] = src[...].astype(BF16)


def _qkv_proj_kernel(x_ref, g_ref, wk_ref, wqvt_ref, *refs, n_cast):
    cast_srcs, (k_ref, qvt_ref), cast_dsts = refs[:n_cast], refs[n_cast:n_cast + 2], refs[n_cast + 2:]
    h = _rms(x_ref[...], g_ref[...], NORM_EPS).astype(BF16)
    k_ref[...] = jnp.dot(h, wk_ref[...], preferred_element_type=F32).astype(BF16)
    qvt = lax.dot_general(wqvt_ref[...], h, _NT, preferred_element_type=F32)
    n_q = N_HEADS * 2 * QK_DIM
    qvt_ref[0:n_q, :] = (qvt[0:n_q] * (LOG2E * QK_DIM ** -0.5)).astype(BF16)
    qvt_ref[n_q:, :] = qvt[n_q:].astype(BF16)
    _run_cast_jobs(cast_srcs, cast_dsts)


def _qkv_proj(x3, g, w_k, w_qvt, cast):
    b, s, d = x3.shape
    tiles = s // ROW_TILE
    n_k, n_qv = w_k.shape[1], w_qvt.shape[0]
    jobs = [_cast_job(w, layer, b * tiles, lambda bi, i: bi * tiles + i) for w, layer in cast]
    return pl.pallas_call(
        functools.partial(_qkv_proj_kernel, n_cast=len(jobs)),
        grid=(b, tiles),
        in_specs=[
            pl.BlockSpec((None, ROW_TILE, d), lambda bi, i: (bi, i, 0)),
            _resident((1, d)),
            _resident((d, n_k)),
            _resident((n_qv, d)),
        ] + [job[0] for job in jobs],
        out_specs=[
            pl.BlockSpec((None, ROW_TILE, n_k), lambda bi, i: (bi, i, 0)),
            pl.BlockSpec((None, n_qv, ROW_TILE), lambda bi, i: (bi, 0, i)),
        ] + [job[1] for job in jobs],
        out_shape=[
            jax.ShapeDtypeStruct((b, s, n_k), BF16),
            jax.ShapeDtypeStruct((b, n_qv, s), BF16),
        ] + [job[2] for job in jobs],
        compiler_params=pltpu.CompilerParams(
            dimension_semantics=("arbitrary", "arbitrary"), vmem_limit_bytes=VMEM_LIMIT),
        name="qkv_proj",
    )(x3, g.reshape(1, d), w_k, w_qvt, *[w for w, _ in cast])


def _attn_kernel(slope_ref, lq1_ref, lk1_ref, lq2_ref, lk2_ref, g_ref, kb_ref, db_ref, k_ref,
                 qt_ref, vt_ref, *refs, lambda_init, n_cast):
    cast_srcs, o_ref, cast_dsts = refs[:n_cast], refs[n_cast], refs[n_cast + 1:2 * n_cast + 1]
    scratch = refs[2 * n_cast + 1:]
    lam = (jnp.exp(jnp.sum(lq1_ref[...] * lk1_ref[...], axis=-1, keepdims=True))
           - jnp.exp(jnp.sum(lq2_ref[...] * lk2_ref[...], axis=-1, keepdims=True))
           + lambda_init)
    heads = [
        _attn_head(slope_ref[pl.program_id(1) * ATT_HEADS + hd], lam, g_ref,
                   kb_ref.at[hd], db_ref.at[hd],
                   k_ref.at[:, hd * 2 * QK_DIM:(hd + 1) * 2 * QK_DIM],
                   qt_ref.at[hd * 2 * QK_DIM:(hd + 1) * 2 * QK_DIM, :],
                   vt_ref.at[hd * V_DIM:(hd + 1) * V_DIM, :],
                   o_ref.at[:, hd * V_DIM:(hd + 1) * V_DIM],
                   *[sc.at[hd] for sc in scratch], lambda_init=lambda_init)
        for hd in range(ATT_HEADS)]
    n_tiles = heads[0][0]
    for idx in range(min(SCORE_LEAD, n_tiles)):
        for _, emit_scores, _ in heads:
            emit_scores(idx)
    for idx in range(n_tiles):
        for _, emit_scores, softmax_pv in heads:
            if idx + SCORE_LEAD < n_tiles:
                emit_scores(idx + SCORE_LEAD)
            softmax_pv(idx)
    _run_cast_jobs(cast_srcs, cast_dsts)


def _attn_head(slope, lam, g_ref, kb_sc, db_sc, k_ref, qt_ref, vt_ref, o_ref, qs_sc, acc_sc,
               s_sc, *, lambda_init):
    t = ATT_TILE
    nq = SEQ // t
    slope = slope * LOG2E
    row = lax.broadcasted_iota(jnp.int32, (2 * QK_DIM, t), 0)

    for qi in range(nq):
        qt = qt_ref[:, qi * t:(qi + 1) * t]
        zero = jnp.zeros_like(qt)
        qs_sc[qi, 0] = jnp.where(row < QK_DIM, qt, zero)
        qs_sc[qi, 1] = jnp.where(row >= QK_DIM, qt, zero)

    tiles = [(kj, qi, mp) for kj in range(nq) for qi in range(kj, nq) for mp in range(2)]

    def emit_scores(idx):
        kj, qi, mp = tiles[idx]
        k_tile = k_ref[kj * t:(kj + 1) * t, :]
        bias = db_sc[...] if qi == kj else kb_sc[...]
        s = jnp.dot(k_tile, qs_sc[qi, mp], preferred_element_type=F32) + bias
        s_sc[idx % S_SLOTS] = s
        s_max[idx] = jnp.max(s, axis=0, keepdims=True)

    s_max = [None] * len(tiles)
    ones_rows = jnp.ones((ONES_ROWS, t), BF16)
    m = [[None, None] for _ in range(nq)]

    def softmax_pv(idx):
        kj, qi, mp = tiles[idx]
        s = s_sc[idx % S_SLOTS]
        v_ext = jnp.concatenate([vt_ref[:, kj * t:(kj + 1) * t], ones_rows], axis=0)
        shift = slope * jnp.full((1, t), float((qi - kj) * t), F32)
        m_tile = s_max[idx] - shift
        if kj == 0:
            m_new = m_tile
            p = jnp.exp2((s - (m_new + shift)).astype(BF16))
            acc_sc[qi, mp] = jnp.dot(v_ext, p, preferred_element_type=F32)
        else:
            m_new = jnp.maximum(m[qi][mp], m_tile)
            alpha = jnp.exp2(m[qi][mp] - m_new)
            p = jnp.exp2((s - (m_new + shift)).astype(BF16))
            acc_sc[qi, mp] = alpha * acc_sc[qi, mp] + jnp.dot(
                v_ext, p, preferred_element_type=F32)
        m[qi][mp] = m_new
        if qi == kj and mp == 1:
            acc1 = acc_sc[qi, 0]
            acc2 = acc_sc[qi, 1]
            o_t = (acc1[0:V_DIM] / acc1[V_DIM:V_DIM + 1]
                   - lam * (acc2[0:V_DIM] / acc2[V_DIM:V_DIM + 1]))
            ms = jnp.mean(o_t * o_t, axis=0, keepdims=True)
            o = (o_t * lax.rsqrt(ms + SUBLN_EPS)).T
            o = o * g_ref[...] * (1.0 - lambda_init)
            o_ref[qi * t:(qi + 1) * t, :] = o.astype(o_ref.dtype)

    return len(tiles), emit_scores, softmax_pv


def _attention(k, qvt, slopes, lq1, lk1, lq2, lk2, subln_g, lambda_init, cast):
    b = k.shape[0]
    t = ATT_TILE
    smem = pl.BlockSpec(memory_space=pltpu.SMEM)
    lam_spec = _resident((1, QK_DIM))
    hd = ATT_HEADS
    steps = N_HEADS // hd
    jobs = [_cast_job(w, layer, b * steps, lambda bi, h: bi * steps + h) for w, layer in cast]
    key = jnp.arange(t, dtype=F32)[:, None]
    qry = jnp.arange(t, dtype=F32)[None, :]
    slope2 = (slopes * LOG2E)[:, None, None]
    kb = slope2 * jnp.broadcast_to(key, (t, t))
    db = jnp.where((key // CHUNK) <= (qry // CHUNK), slope2 * (qry - jnp.abs(qry - key)),
                   MASK_VALUE)
    table_spec = pl.BlockSpec((hd, t, t), lambda bi, h: (h, 0, 0))
    return pl.pallas_call(
        functools.partial(_attn_kernel, lambda_init=lambda_init, n_cast=len(jobs)),
        grid=(b, steps),
        in_specs=[
            smem, lam_spec, lam_spec, lam_spec, lam_spec, _resident((1, V_DIM)),
            table_spec, table_spec,
            pl.BlockSpec((None, SEQ, hd * 2 * QK_DIM), lambda bi, h: (bi, 0, h)),
            pl.BlockSpec((None, hd * 2 * QK_DIM, SEQ), lambda bi, h: (bi, h, 0)),
            pl.BlockSpec((None, hd * V_DIM, SEQ), lambda bi, h: (bi, steps + h, 0)),
        ] + [job[0] for job in jobs],
        out_specs=[pl.BlockSpec((None, SEQ, hd * V_DIM), lambda bi, h: (bi, 0, h))]
        + [job[1] for job in jobs],
        out_shape=[jax.ShapeDtypeStruct((b, SEQ, N_HEADS * V_DIM), BF16)]
        + [job[2] for job in jobs],
        scratch_shapes=[
            pltpu.VMEM((hd, SEQ // t, 2, 2 * QK_DIM, t), BF16),
            pltpu.VMEM((hd, SEQ // t, 2, V_DIM + ONES_ROWS, t), F32),
            pltpu.VMEM((hd, S_SLOTS, t, t), F32),
        ],
        compiler_params=pltpu.CompilerParams(
            dimension_semantics=("arbitrary", "arbitrary"), vmem_limit_bytes=VMEM_LIMIT),
        name="diff_attention",
    )(slopes, lq1.reshape(1, QK_DIM), lk1.reshape(1, QK_DIM), lq2.reshape(1, QK_DIM),
      lk2.reshape(1, QK_DIM), subln_g.reshape(1, V_DIM), kb, db, k, qvt, qvt,
      *[w for w, _ in cast])


def _causal_conv3(u_sc, halo_ref, u, cw, first, rows):
    hist = halo_ref[...]
    u_sc[0:HALO, :] = jnp.where(first, jnp.zeros_like(hist), hist)
    u_sc[HALO:HALO + rows, :] = u
    halo_ref[...] = u[rows - HALO:rows, :]
    return (u * cw[2:3, :]
            + u_sc[HALO - 1:HALO - 1 + rows, :] * cw[1:2, :]
            + u_sc[HALO - 2:HALO - 2 + rows, :] * cw[0:1, :])


def _ffn_tile(x, first, gpre_ref, gpost_ref, wup_ref, cw_ref, wdn_ref,
              h_sc, u_sc, halo_sc, act_sc):
    rows = x.shape[0]
    c = FFN_CHUNK
    h_sc[...] = _rms(x, gpre_ref[...], NORM_EPS).astype(BF16)
    for j in range(D_FF // c):
        gs = slice(c * j, c * (j + 1))
        vs = slice(D_FF + c * j, D_FF + c * (j + 1))
        h = h_sc[...]
        u = jnp.concatenate(
            [jnp.dot(h, wup_ref[:, gs], preferred_element_type=F32),
             jnp.dot(h, wup_ref[:, vs], preferred_element_type=F32)], axis=1)
        cw = jnp.concatenate([cw_ref[:, gs], cw_ref[:, vs]], axis=1)
        y = _causal_conv3(u_sc, halo_sc.at[j], u, cw, first, rows)
        gate = y[:, 0:c]
        act_sc[:, gs] = (gate * jax.nn.sigmoid(gate) * y[:, c:2 * c]).astype(BF16)
    m = jnp.dot(act_sc[...], wdn_ref[...], preferred_element_type=F32)
    return x + _rms(m, gpost_ref[...], NORM_EPS)


def _ffn_kernel(x_ref, gpre_ref, gpost_ref, wup_ref, cw_ref, wdn_ref, o_ref, *scratch):
    first = (pl.program_id(0) % (SEQ // x_ref.shape[0])) == 0
    o_ref[...] = _ffn_tile(x_ref[...], first, gpre_ref, gpost_ref, wup_ref, cw_ref, wdn_ref,
                           *scratch)


def _oproj_ffn_kernel(a_ref, wo_ref, gmix_ref, x_ref, gpre_ref, gpost_ref, wup_ref, cw_ref,
                      wdn_ref, o_ref, *scratch):
    first = (pl.program_id(0) % (SEQ // x_ref.shape[0])) == 0
    mix = jnp.dot(a_ref[...], wo_ref[...], preferred_element_type=F32)
    x = x_ref[...] + _rms(mix, gmix_ref[...], NORM_EPS)
    o_ref[...] = _ffn_tile(x, first, gpre_ref, gpost_ref, wup_ref, cw_ref, wdn_ref, *scratch)


def _ffn(x2, g_pre, g_post, w_up, conv_w, w_down, attn=None):
    n_rows, d = x2.shape
    c = FFN_CHUNK
    row_spec = pl.BlockSpec((ROW_TILE, d), lambda i: (i, 0))
    in_specs = [
        row_spec,
        _resident((1, d)),
        _resident((1, d)),
        _resident((d, 2 * D_FF)),
        _resident((CONV_WIDTH, 2 * D_FF)),
        _resident((D_FF, d)),
    ]
    args = [x2, g_pre.reshape(1, d), g_post.reshape(1, d), w_up, conv_w, w_down]
    body = _ffn_kernel
    if attn is not None:
        a2, w_o, g_mix = attn
        in_specs = [pl.BlockSpec((ROW_TILE, a2.shape[1]), lambda i: (i, 0)),
                    _resident(w_o.shape), _resident((1, d))] + in_specs
        args = [a2, w_o, g_mix.reshape(1, d)] + args
        body = _oproj_ffn_kernel
    return pl.pallas_call(
        body,
        grid=(n_rows // ROW_TILE,),
        in_specs=in_specs,
        out_specs=row_spec,
        out_shape=jax.ShapeDtypeStruct((n_rows, d), F32),
        scratch_shapes=[
            pltpu.VMEM((ROW_TILE, d), BF16),
            pltpu.VMEM((HALO + ROW_TILE, 2 * c), F32),
            pltpu.VMEM((D_FF // c, HALO, 2 * c), F32),
            pltpu.VMEM((ROW_TILE, D_FF), BF16),
        ],
        compiler_params=pltpu.CompilerParams(
            dimension_semantics=("arbitrary",), vmem_limit_bytes=VMEM_LIMIT),
        name="conv_glu_ffn",
    )(*args)


def _mixer_kernel(x_ref, gpre_ref, gpost_ref, win_ref, cw_ref, wout_ref, o_ref,
                  h_sc, u_sc, halo_sc, y_sc):
    rows = x_ref.shape[0]
    c = MIX_CHUNK
    first = (pl.program_id(0) % (SEQ // rows)) == 0
    x = x_ref[...]
    h_sc[...] = _rms(x, gpre_ref[...], NORM_EPS).astype(BF16)
    for j in range(D_MODEL // c):
        cs = slice(c * j, c * (j + 1))
        h = h_sc[...]
        b_gate, c_gate, hv = (
            jnp.dot(h, win_ref[:, part * D_MODEL + c * j:part * D_MODEL + c * (j + 1)],
                    preferred_element_type=F32) for part in range(3))
        y = _causal_conv3(u_sc, halo_sc.at[j], c_gate * hv, cw_ref[:, cs], first, rows)
        y_sc[:, cs] = (b_gate * y).astype(BF16)
    m = jnp.dot(y_sc[...], wout_ref[...], preferred_element_type=F32)
    o_ref[...] = x + _rms(m, gpost_ref[...], NORM_EPS)


def _mixer(x2, g_pre, g_post, w_in, conv_w, w_out):
    n_rows, d = x2.shape
    c = MIX_CHUNK
    return pl.pallas_call(
        _mixer_kernel,
        grid=(n_rows // ROW_TILE,),
        in_specs=[
            pl.BlockSpec((ROW_TILE, d), lambda i: (i, 0)),
            _resident((1, d)),
            _resident((1, d)),
            _resident((d, 3 * d)),
            _resident((CONV_WIDTH, d)),
            _resident((d, d)),
        ],
        out_specs=pl.BlockSpec((ROW_TILE, d), lambda i: (i, 0)),
        out_shape=jax.ShapeDtypeStruct((n_rows, d), F32),
        scratch_shapes=[
            pltpu.VMEM((ROW_TILE, d), BF16),
            pltpu.VMEM((HALO + ROW_TILE, c), F32),
            pltpu.VMEM((d // c, HALO, c), F32),
            pltpu.VMEM((ROW_TILE, d), BF16),
        ],
        compiler_params=pltpu.CompilerParams(
            dimension_semantics=("arbitrary",), vmem_limit_bytes=VMEM_LIMIT),
        name="conv_mixer",
    )(x2, g_pre.reshape(1, d), g_post.reshape(1, d), w_in, conv_w, w_out)


def _qkv_layout(w):
    d = w.shape[0]
    qk = N_HEADS * QK_DIM
    q1, q2, k1, k2 = (w[:, i * qk:(i + 1) * qk].reshape(d, N_HEADS, QK_DIM) for i in range(4))
    w_k = jnp.concatenate([k1, k2], axis=-1).reshape(d, 2 * qk).astype(BF16)
    w_q = jnp.concatenate([q1, q2], axis=-1).reshape(d, 2 * qk)
    w_qvt = jnp.concatenate([w_q, w[:, 4 * qk:]], axis=1).T.astype(BF16)
    return w_k, w_qvt


def kernel(x, norm_g, attn_w_qkv, attn_w_o, attn_lambda_q1, attn_lambda_k1, attn_lambda_q2,
           attn_lambda_k2, attn_subln_g, conv_w_in, conv_w, conv_w_out, ffn_w_up, ffn_conv_w,
           ffn_w_down):
    b, s, d = x.shape
    assert (s, d) == (SEQ, D_MODEL) and norm_g.shape == (2, 4, d)
    assert attn_w_qkv.shape[0] == 1 and conv_w_in.shape[0] == 1 and ffn_w_up.shape[0] == 2
    slopes = jnp.exp2(-8.0 * jnp.arange(1, N_HEADS + 1, dtype=F32) / N_HEADS)
    x2 = x.reshape(b * s, d)

    g = norm_g[0]
    lambda_init = 0.8 - 0.6 * math.exp(-0.3 * 0)
    w_k, w_qvt = _qkv_layout(attn_w_qkv[0])
    k, qvt, w_o, w_up0, w_dn0 = _qkv_proj(
        x2.reshape(b, s, d), g[0], w_k, w_qvt,
        cast=[(attn_w_o, 0), (ffn_w_up, 0), (ffn_w_down, 0)])
    o, w_in, w_out, w_up1, w_dn1 = _attention(
        k, qvt, slopes, attn_lambda_q1[0], attn_lambda_k1[0], attn_lambda_q2[0],
        attn_lambda_k2[0], attn_subln_g[0], lambda_init,
        cast=[(conv_w_in, 0), (conv_w_out, 0), (ffn_w_up, 1), (ffn_w_down, 1)])
    x2 = _ffn(x2, g[2], g[3], w_up0, ffn_conv_w[0], w_dn0,
              attn=(o.reshape(b * s, d), w_o, g[1]))

    g = norm_g[1]
    x2 = _mixer(x2, g[0], g[1], w_in, conv_w[0], w_out)
    x2 = _ffn(x2, g[2], g[3], w_up1, ffn_conv_w[1], w_dn1)
    return x2.reshape(b, s, d)
```

```python
import functools
import math

import jax
import jax.numpy as jnp
from jax import lax
from jax.experimental import pallas as pl
from jax.experimental.pallas import tpu as pltpu

D_MODEL = 1024
SEQ = 2048
CHUNK = 64
N_HEADS = 8
QK_DIM = 64
V_DIM = 128
QK_COLS = N_HEADS * 4 * QK_DIM
CONV_WIDTH = 3
D_FF = 2816
NORM_EPS = 1e-6
SUBLN_EPS = 1e-5

F32 = jnp.float32
BF16 = jnp.bfloat16

ROW_TILE = 512
ATT_TILE = 256
ATT_HEADS = 4
SCORE_LEAD = 4
S_SLOTS = SCORE_LEAD + 1
BF16_SUBLANES = 16
ONES_ROWS = BF16_SUBLANES
FFN_CHUNK = 256
MIX_CHUNK = 256
HALO = 8
VMEM_LIMIT = 56 * 1024 * 1024
MASK_VALUE = -1e30
LOG2E = math.log2(math.e)

_NT = (((1,), (1,)), ((), ()))


def _rms(x, g, eps):
    ms = jnp.mean(x * x, axis=-1, keepdims=True)
    return x * lax.rsqrt(ms + eps) * g


def _resident(shape):
    nd = len(shape)
    return pl.BlockSpec(shape, lambda *_: (0,) * nd, pipeline_mode=pl.Buffered(1))


def _cast_job(w, layer, n_steps, step_of):
    rows, cols = w.shape[-2:]
    n_blocks = n_steps
    while rows % n_blocks or (rows // n_blocks) % BF16_SUBLANES:
        n_blocks //= 2
    every = n_steps // n_blocks
    rb = rows // n_blocks
    in_spec = pl.BlockSpec((None, rb, cols), lambda *g: (layer, step_of(*g) // every, 0))
    out_spec = pl.BlockSpec((rb, cols), lambda *g: (step_of(*g) // every, 0))
    return in_spec, out_spec, jax.ShapeDtypeStruct((rows, cols), BF16)


def _run_cast_jobs(srcs, dsts):
    for src, dst in zip(srcs, dsts):
        dst[...] = src[...].astype(BF16)


def _qkv_proj_kernel(x_ref, g_ref, wk_ref, wqvt_ref, *refs, n_cast):
    cast_srcs, (k_ref, qvt_ref), cast_dsts = refs[:n_cast], refs[n_cast:n_cast + 2], refs[n_cast + 2:]
    h = _rms(x_ref[...], g_ref[...], NORM_EPS).astype(BF16)
    k_ref[...] = jnp.dot(h, wk_ref[...], preferred_element_type=F32).astype(BF16)
    qvt = lax.dot_general(wqvt_ref[...], h, _NT, preferred_element_type=F32)
    n_q = N_HEADS * 2 * QK_DIM
    qvt_ref[0:n_q, :] = (qvt[0:n_q] * (LOG2E * QK_DIM ** -0.5)).astype(BF16)
    qvt_ref[n_q:, :] = qvt[n_q:].astype(BF16)
    _run_cast_jobs(cast_srcs, cast_dsts)


def _qkv_proj(x3, g, w_k, w_qvt, cast):
    b, s, d = x3.shape
    tiles = s // ROW_TILE
    n_k, n_qv = w_k.shape[1], w_qvt.shape[0]
    jobs = [_cast_job(w, layer, b * tiles, lambda bi, i: bi * tiles + i) for w, layer in cast]
    return pl.pallas_call(
        functools.partial(_qkv_proj_kernel, n_cast=len(jobs)),
        grid=(b, tiles),
        in_specs=[
            pl.BlockSpec((None, ROW_TILE, d), lambda bi, i: (bi, i, 0)),
            _resident((1, d)),
            _resident((d, n_k)),
            _resident((n_qv, d)),
        ] + [job[0] for job in jobs],
        out_specs=[
            pl.BlockSpec((None, ROW_TILE, n_k), lambda bi, i: (bi, i, 0)),
            pl.BlockSpec((None, n_qv, ROW_TILE), lambda bi, i: (bi, 0, i)),
        ] + [job[1] for job in jobs],
        out_shape=[
            jax.ShapeDtypeStruct((b, s, n_k), BF16),
            jax.ShapeDtypeStruct((b, n_qv, s), BF16),
        ] + [job[2] for job in jobs],
        compiler_params=pltpu.CompilerParams(
            dimension_semantics=("arbitrary", "arbitrary"), vmem_limit_bytes=VMEM_LIMIT),
        name="qkv_proj",
    )(x3, g.reshape(1, d), w_k, w_qvt, *[w for w, _ in cast])


def _attn_kernel(slope_ref, lq1_ref, lk1_ref, lq2_ref, lk2_ref, g_ref, kb_ref, db_ref, k_ref,
                 qt_ref, vt_ref, *refs, lambda_init, n_cast):
    cast_srcs, o_ref, cast_dsts = refs[:n_cast], refs[n_cast], refs[n_cast + 1:2 * n_cast + 1]
    scratch = refs[2 * n_cast + 1:]
    lam = (jnp.exp(jnp.sum(lq1_ref[...] * lk1_ref[...], axis=-1, keepdims=True))
           - jnp.exp(jnp.sum(lq2_ref[...] * lk2_ref[...], axis=-1, keepdims=True))
           + lambda_init)
    heads = [
        _attn_head(slope_ref[pl.program_id(1) * ATT_HEADS + hd], lam, g_ref,
                   kb_ref.at[hd], db_ref.at[hd],
                   k_ref.at[:, hd * 2 * QK_DIM:(hd + 1) * 2 * QK_DIM],
                   qt_ref.at[hd * 2 * QK_DIM:(hd + 1) * 2 * QK_DIM, :],
                   vt_ref.at[hd * V_DIM:(hd + 1) * V_DIM, :],
                   o_ref.at[:, hd * V_DIM:(hd + 1) * V_DIM],
                   *[sc.at[hd] for sc in scratch], lambda_init=lambda_init)
        for hd in range(ATT_HEADS)]
    n_tiles = heads[0][0]
    for idx in range(min(SCORE_LEAD, n_tiles)):
        for _, emit_scores, _ in heads:
            emit_scores(idx)
    for idx in range(n_tiles):
        for _, emit_scores, softmax_pv in heads:
            if idx + SCORE_LEAD < n_tiles:
                emit_scores(idx + SCORE_LEAD)
            softmax_pv(idx)
    _run_cast_jobs(cast_srcs, cast_dsts)


def _attn_head(slope, lam, g_ref, kb_sc, db_sc, k_ref, qt_ref, vt_ref, o_ref, qs_sc, acc_sc,
               s_sc, *, lambda_init):
    t = ATT_TILE
    nq = SEQ // t
    slope = slope * LOG2E
    row = lax.broadcasted_iota(jnp.int32, (2 * QK_DIM, t), 0)

    for qi in range(nq):
        qt = qt_ref[:, qi * t:(qi + 1) * t]
        zero = jnp.zeros_like(qt)
        qs_sc[qi, 0] = jnp.where(row < QK_DIM, qt, zero)
        qs_sc[qi, 1] = jnp.where(row >= QK_DIM, qt, zero)

    tiles = [(kj, qi, mp) for kj in range(nq) for qi in range(kj, nq) for mp in range(2)]

    def emit_scores(idx):
        kj, qi, mp = tiles[idx]
        k_tile = k_ref[kj * t:(kj + 1) * t, :]
        bias = db_sc[...] if qi == kj else kb_sc[...]
        s = jnp.dot(k_tile, qs_sc[qi, mp], preferred_element_type=F32) + bias
        s_sc[idx % S_SLOTS] = s
        s_max[idx] = jnp.max(s, axis=0, keepdims=True)

    s_max = [None] * len(tiles)
    ones_rows = jnp.ones((ONES_ROWS, t), BF16)
    m = [[None, None] for _ in range(nq)]

    def softmax_pv(idx):
        kj, qi, mp = tiles[idx]
        s = s_sc[idx % S_SLOTS]
        v_ext = jnp.concatenate([vt_ref[:, kj * t:(kj + 1) * t], ones_rows], axis=0)
        shift = slope * jnp.full((1, t), float((qi - kj) * t), F32)
        m_tile = s_max[idx] - shift
        if kj == 0:
            m_new = m_tile
            p = jnp.exp2((s - (m_new + shift)).astype(BF16))
            acc_sc[qi, mp] = jnp.dot(v_ext, p, preferred_element_type=F32)
        else:
            m_new = jnp.maximum(m[qi][mp], m_tile)
            alpha = jnp.exp2(m[qi][mp] - m_new)
            p = jnp.exp2((s - (m_new + shift)).astype(BF16))
            acc_sc[qi, mp] = alpha * acc_sc[qi, mp] + jnp.dot(
                v_ext, p, preferred_element_type=F32)
        m[qi][mp] = m_new
        if qi == kj and mp == 1:
            acc1 = acc_sc[qi, 0]
            acc2 = acc_sc[qi, 1]
            o_t = (acc1[0:V_DIM] / acc1[V_DIM:V_DIM + 1]
                   - lam * (acc2[0:V_DIM] / acc2[V_DIM:V_DIM + 1]))
            ms = jnp.mean(o_t * o_t, axis=0, keepdims=True)
            o = (o_t * lax.rsqrt(ms + SUBLN_EPS)).T
            o = o * g_ref[...] * (1.0 - lambda_init)
            o_ref[qi * t:(qi + 1) * t, :] = o.astype(o_ref.dtype)

    return len(tiles), emit_scores, softmax_pv


def _attention(k, qvt, slopes, lq1, lk1, lq2, lk2, subln_g, lambda_init, cast):
    b = k.shape[0]
    t = ATT_TILE
    smem = pl.BlockSpec(memory_space=pltpu.SMEM)
    lam_spec = _resident((1, QK_DIM))
    hd = ATT_HEADS
    steps = N_HEADS // hd
    jobs = [_cast_job(w, layer, b * steps, lambda bi, h: bi * steps + h) for w, layer in cast]
    key = jnp.arange(t, dtype=F32)[:, None]
    qry = jnp.arange(t, dtype=F32)[None, :]
    slope2 = (slopes * LOG2E)[:, None, None]
    kb = slope2 * jnp.broadcast_to(key, (t, t))
    db = jnp.where((key // CHUNK) <= (qry // CHUNK), slope2 * (qry - jnp.abs(qry - key)),
                   MASK_VALUE)
    table_spec = pl.BlockSpec((hd, t, t), lambda bi, h: (h, 0, 0))
    return pl.pallas_call(
        functools.partial(_attn_kernel, lambda_init=lambda_init, n_cast=len(jobs)),
        grid=(b, steps),
        in_specs=[
            smem, lam_spec, lam_spec, lam_spec, lam_spec, _resident((1, V_DIM)),
            table_spec, table_spec,
            pl.BlockSpec((None, SEQ, hd * 2 * QK_DIM), lambda bi, h: (bi, 0, h)),
            pl.BlockSpec((None, hd * 2 * QK_DIM, SEQ), lambda bi, h: (bi, h, 0)),
            pl.BlockSpec((None, hd * V_DIM, SEQ), lambda bi, h: (bi, steps + h, 0)),
        ] + [job[0] for job in jobs],
        out_specs=[pl.BlockSpec((None, SEQ, hd * V_DIM), lambda bi, h: (bi, 0, h))]
        + [job[1] for job in jobs],
        out_shape=[jax.ShapeDtypeStruct((b, SEQ, N_HEADS * V_DIM), BF16)]
        + [job[2] for job in jobs],
        scratch_shapes=[
            pltpu.VMEM((hd, SEQ // t, 2, 2 * QK_DIM, t), BF16),
            pltpu.VMEM((hd, SEQ // t, 2, V_DIM + ONES_ROWS, t), F32),
            pltpu.VMEM((hd, S_SLOTS, t, t), F32),
        ],
        compiler_params=pltpu.CompilerParams(
            dimension_semantics=("arbitrary", "arbitrary"), vmem_limit_bytes=VMEM_LIMIT),
        name="diff_attention",
    )(slopes, lq1.reshape(1, QK_DIM), lk1.reshape(1, QK_DIM), lq2.reshape(1, QK_DIM),
      lk2.reshape(1, QK_DIM), subln_g.reshape(1, V_DIM), kb, db, k, qvt, qvt,
      *[w for w, _ in cast])


def _causal_conv3(u_sc, halo_ref, u, cw, first, rows):
    hist = halo_ref[...]
    u_sc[0:HALO, :] = jnp.where(first, jnp.zeros_like(hist), hist)
    u_sc[HALO:HALO + rows, :] = u
    halo_ref[...] = u[rows - HALO:rows, :]
    return (u * cw[2:3, :]
            + u_sc[HALO - 1:HALO - 1 + rows, :] * cw[1:2, :]
            + u_sc[HALO - 2:HALO - 2 + rows, :] * cw[0:1, :])


def _ffn_tile(x, first, gpre_ref, gpost_ref, wup_ref, cw_ref, wdn_ref,
              h_sc, u_sc, halo_sc, act_sc):
    rows = x.shape[0]
    c = FFN_CHUNK
    h_sc[...] = _rms(x, gpre_ref[...], NORM_EPS).astype(BF16)
    for j in range(D_FF // c):
        gs = slice(c * j, c * (j + 1))
        vs = slice(D_FF + c * j, D_FF + c * (j + 1))
        h = h_sc[...]
        u = jnp.concatenate(
            [jnp.dot(h, wup_ref[:, gs], preferred_element_type=F32),
             jnp.dot(h, wup_ref[:, vs], preferred_element_type=F32)], axis=1)
        cw = jnp.concatenate([cw_ref[:, gs], cw_ref[:, vs]], axis=1)
        y = _causal_conv3(u_sc, halo_sc.at[j], u, cw, first, rows)
        gate = y[:, 0:c]
        act_sc[:, gs] = (gate * jax.nn.sigmoid(gate) * y[:, c:2 * c]).astype(BF16)
    m = jnp.dot(act_sc[...], wdn_ref[...], preferred_element_type=F32)
    return x + _rms(m, gpost_ref[...], NORM_EPS)


def _ffn_kernel(x_ref, gpre_ref, gpost_ref, wup_ref, cw_ref, wdn_ref, o_ref, *scratch):
    first = (pl.program_id(0) % (SEQ // x_ref.shape[0])) == 0
    o_ref[...] = _ffn_tile(x_ref[...], first, gpre_ref, gpost_ref, wup_ref, cw_ref, wdn_ref,
                           *scratch)


def _oproj_ffn_kernel(a_ref, wo_ref, gmix_ref, x_ref, gpre_ref, gpost_ref, wup_ref, cw_ref,
                      wdn_ref, o_ref, *scratch):
    first = (pl.program_id(0) % (SEQ // x_ref.shape[0])) == 0
    mix = jnp.dot(a_ref[...], wo_ref[...], preferred_element_type=F32)
    x = x_ref[...] + _rms(mix, gmix_ref[...], NORM_EPS)
    o_ref[...] = _ffn_tile(x, first, gpre_ref, gpost_ref, wup_ref, cw_ref, wdn_ref, *scratch)


def _ffn(x2, g_pre, g_post, w_up, conv_w, w_down, attn=None):
    n_rows, d = x2.shape
    c = FFN_CHUNK
    row_spec = pl.BlockSpec((ROW_TILE, d), lambda i: (i, 0))
    in_specs = [
        row_spec,
        _resident((1, d)),
        _resident((1, d)),
        _resident((d, 2 * D_FF)),
        _resident((CONV_WIDTH, 2 * D_FF)),
        _resident((D_FF, d)),
    ]
    args = [x2, g_pre.reshape(1, d), g_post.reshape(1, d), w_up, conv_w, w_down]
    body = _ffn_kernel
    if attn is not None:
        a2, w_o, g_mix = attn
        in_specs = [pl.BlockSpec((ROW_TILE, a2.shape[1]), lambda i: (i, 0)),
                    _resident(w_o.shape), _resident((1, d))] + in_specs
        args = [a2, w_o, g_mix.reshape(1, d)] + args
        body = _oproj_ffn_kernel
    return pl.pallas_call(
        body,
        grid=(n_rows // ROW_TILE,),
        in_specs=in_specs,
        out_specs=row_spec,
        out_shape=jax.ShapeDtypeStruct((n_rows, d), F32),
        scratch_shapes=[
            pltpu.VMEM((ROW_TILE, d), BF16),
            pltpu.VMEM((HALO + ROW_TILE, 2 * c), F32),
            pltpu.VMEM((D_FF // c, HALO, 2 * c), F32),
            pltpu.VMEM((ROW_TILE, D_FF), BF16),
        ],
        compiler_params=pltpu.CompilerParams(
            dimension_semantics=("arbitrary",), vmem_limit_bytes=VMEM_LIMIT),
        name="conv_glu_ffn",
    )(*args)


def _mixer_kernel(x_ref, gpre_ref, gpost_ref, win_ref, cw_ref, wout_ref, o_ref,
                  h_sc, u_sc, halo_sc, y_sc):
    rows = x_ref.shape[0]
    c = MIX_CHUNK
    first = (pl.program_id(0) % (SEQ // rows)) == 0
    x = x_ref[...]
    h_sc[...] = _rms(x, gpre_ref[...], NORM_EPS).astype(BF16)
    for j in range(D_MODEL // c):
        cs = slice(c * j, c * (j + 1))
        h = h_sc[...]
        b_gate, c_gate, hv = (
            jnp.dot(h, win_ref[:, part * D_MODEL + c * j:part * D_MODEL + c * (j + 1)],
                    preferred_element_type=F32) for part in range(3))
        y = _causal_conv3(u_sc, halo_sc.at[j], c_gate * hv, cw_ref[:, cs], first, rows)
        y_sc[:, cs] = (b_gate * y).astype(BF16)
    m = jnp.dot(y_sc[...], wout_ref[...], preferred_element_type=F32)
    o_ref[...] = x + _rms(m, gpost_ref[...], NORM_EPS)


def _mixer(x2, g_pre, g_post, w_in, conv_w, w_out):
    n_rows, d = x2.shape
    c = MIX_CHUNK
    return pl.pallas_call(
        _mixer_kernel,
        grid=(n_rows // ROW_TILE,),
        in_specs=[
            pl.BlockSpec((ROW_TILE, d), lambda i: (i, 0)),
            _resident((1, d)),
            _resident((1, d)),
            _resident((d, 3 * d)),
            _resident((CONV_WIDTH, d)),
            _resident((d, d)),
        ],
        out_specs=pl.BlockSpec((ROW_TILE, d), lambda i: (i, 0)),
        out_shape=jax.ShapeDtypeStruct((n_rows, d), F32),
        scratch_shapes=[
            pltpu.VMEM((ROW_TILE, d), BF16),
            pltpu.VMEM((HALO + ROW_TILE, c), F32),
            pltpu.VMEM((d // c, HALO, c), F32),
            pltpu.VMEM((ROW_TILE, d), BF16),
        ],
        compiler_params=pltpu.CompilerParams(
            dimension_semantics=("arbitrary",), vmem_limit_bytes=VMEM_LIMIT),
        name="conv_mixer",
    )(x2, g_pre.reshape(1, d), g_post.reshape(1, d), w_in, conv_w, w_out)


def _qkv_layout(w):
    d = w.shape[0]
    qk = N_HEADS * QK_DIM
    q1, q2, k1, k2 = (w[:, i * qk:(i + 1) * qk].reshape(d, N_HEADS, QK_DIM) for i in range(4))
    w_k = jnp.concatenate([k1, k2], axis=-1).reshape(d, 2 * qk).astype(BF16)
    w_q = jnp.concatenate([q1, q2], axis=-1).reshape(d, 2 * qk)
    w_qvt = jnp.concatenate([w_q, w[:, 4 * qk:]], axis=1).T.astype(BF16)
    return w_k, w_qvt


def kernel(x, norm_g, attn_w_qkv, attn_w_o, attn_lambda_q1, attn_lambda_k1, attn_lambda_q2,
           attn_lambda_k2, attn_subln_g, conv_w_in, conv_w, conv_w_out, ffn_w_up, ffn_conv_w,
           ffn_w_down):
    b, s, d = x.shape
    assert (s, d) == (SEQ, D_MODEL) and norm_g.shape == (2, 4, d)
    assert attn_w_qkv.shape[0] == 1 and conv_w_in.shape[0] == 1 and ffn_w_up.shape[0] == 2
    slopes = jnp.exp2(-8.0 * jnp.arange(1, N_HEADS + 1, dtype=F32) / N_HEADS)
    x2 = x.reshape(b * s, d)

    g = norm_g[0]
    lambda_init = 0.8 - 0.6 * math.exp(-0.3 * 0)
    w_k, w_qvt = _qkv_layout(attn_w_qkv[0])
    k, qvt, w_o, w_up0, w_dn0 = _qkv_proj(
        x2.reshape(b, s, d), g[0], w_k, w_qvt,
        cast=[(attn_w_o, 0), (ffn_w_up, 0), (ffn_w_down, 0)])
    o, w_in, w_out, w_up1, w_dn1 = _attention(
        k, qvt, slopes, attn_lambda_q1[0], attn_lambda_k1[0], attn_lambda_q2[0],
        attn_lambda_k2[0], attn_subln_g[0], lambda_init,
        cast=[(conv_w_in, 0), (conv_w_out, 0), (ffn_w_up, 1), (ffn_w_down, 1)])
    x2 = _ffn(x2, g[2], g[3], w_up0, ffn_conv_w[0], w_dn0,
              attn=(o.reshape(b * s, d), w_o, g[1]))

    g = norm_g[1]
    x2 = _mixer(x2, g[0], g[1], w_in, conv_w[0], w_out)
    x2 = _ffn(x2, g[2], g[3], w_up1, ffn_conv_w[1], w_dn1)
    return x2.reshape(b, s, d)
```

```python
import functools
import math

import jax
import jax.numpy as jnp
from jax import lax
from jax.experimental import pallas as pl
from jax.experimental.pallas import tpu as pltpu

D_MODEL = 1024
SEQ = 2048
CHUNK = 64
N_HEADS = 8
QK_DIM = 64
V_DIM = 128
QK_COLS = N_HEADS * 4 * QK_DIM
CONV_WIDTH = 3
D_FF = 2816
NORM_EPS = 1e-6
SUBLN_EPS = 1e-5

F32 = jnp.float32
BF16 = jnp.bfloat16

ROW_TILE = 512
ATT_TILE = 256
ATT_HEADS = 2
SCORE_LEAD = 2
S_SLOTS = SCORE_LEAD + 1
BF16_SUBLANES = 16
ONES_ROWS = BF16_SUBLANES
FFN_CHUNK = 256
MIX_CHUNK = 256
HALO = 8
VMEM_LIMIT = 56 * 1024 * 1024
MASK_VALUE = -1e30
LOG2E = math.log2(math.e)

_NT = (((1,), (1,)), ((), ()))


def _rms(x, g, eps):
    ms = jnp.mean(x * x, axis=-1, keepdims=True)
    return x * lax.rsqrt(ms + eps) * g


def _resident(shape):
    nd = len(shape)
    return pl.BlockSpec(shape, lambda *_: (0,) * nd, pipeline_mode=pl.Buffered(1))


def _cast_job(w, layer, n_steps, step_of):
    rows, cols = w.shape[-2:]
    n_blocks = n_steps
    while rows % n_blocks or (rows // n_blocks) % BF16_SUBLANES:
        n_blocks //= 2
    every = n_steps // n_blocks
    rb = rows // n_blocks
    in_spec = pl.BlockSpec((None, rb, cols), lambda *g: (layer, step_of(*g) // every, 0))
    out_spec = pl.BlockSpec((rb, cols), lambda *g: (step_of(*g) // every, 0))
    return in_spec, out_spec, jax.ShapeDtypeStruct((rows, cols), BF16)


def _run_cast_jobs(srcs, dsts):
    for src, dst in zip(srcs, dsts):
        dst[...] = src[...].astype(BF16)


def _qkv_proj_kernel(x_ref, g_ref, wk_ref, wqvt_ref, *refs, n_cast):
    cast_srcs, (k_ref, qvt_ref), cast_dsts = refs[:n_cast], refs[n_cast:n_cast + 2], refs[n_cast + 2:]
    h = _rms(x_ref[...], g_ref[...], NORM_EPS).astype(BF16)
    k_ref[...] = jnp.dot(h, wk_ref[...], preferred_element_type=F32).astype(BF16)
    qvt = lax.dot_general(wqvt_ref[...], h, _NT, preferred_element_type=F32)
    n_q = N_HEADS * 2 * QK_DIM
    qvt_ref[0:n_q, :] = (qvt[0:n_q] * (LOG2E * QK_DIM ** -0.5)).astype(BF16)
    qvt_ref[n_q:, :] = qvt[n_q:].astype(BF16)
    _run_cast_jobs(cast_srcs, cast_dsts)


def _qkv_proj(x3, g, w_k, w_qvt, cast):
    b, s, d = x3.shape
    tiles = s // ROW_TILE
    n_k, n_qv = w_k.shape[1], w_qvt.shape[0]
    jobs = [_cast_job(w, layer, b * tiles, lambda bi, i: bi * tiles + i) for w, layer in cast]
    return pl.pallas_call(
        functools.partial(_qkv_proj_kernel, n_cast=len(jobs)),
        grid=(b, tiles),
        in_specs=[
            pl.BlockSpec((None, ROW_TILE, d), lambda bi, i: (bi, i, 0)),
            _resident((1, d)),
            _resident((d, n_k)),
            _resident((n_qv, d)),
        ] + [job[0] for job in jobs],
        out_specs=[
            pl.BlockSpec((None, ROW_TILE, n_k), lambda bi, i: (bi, i, 0)),
            pl.BlockSpec((None, n_qv, ROW_TILE), lambda bi, i: (bi, 0, i)),
        ] + [job[1] for job in jobs],
        out_shape=[
            jax.ShapeDtypeStruct((b, s, n_k), BF16),
            jax.ShapeDtypeStruct((b, n_qv, s), BF16),
        ] + [job[2] for job in jobs],
        compiler_params=pltpu.CompilerParams(
            dimension_semantics=("arbitrary", "arbitrary"), vmem_limit_bytes=VMEM_LIMIT),
        name="qkv_proj",
    )(x3, g.reshape(1, d), w_k, w_qvt, *[w for w, _ in cast])


def _attn_kernel(slope_ref, lq1_ref, lk1_ref, lq2_ref, lk2_ref, g_ref, kb_ref, db_ref, k_ref,
                 qt_ref, vt_ref, *refs, lambda_init, n_cast):
    cast_srcs, o_ref, cast_dsts = refs[:n_cast], refs[n_cast], refs[n_cast + 1:2 * n_cast + 1]
    scratch = refs[2 * n_cast + 1:]
    lam = (jnp.exp(jnp.sum(lq1_ref[...] * lk1_ref[...], axis=-1, keepdims=True))
           - jnp.exp(jnp.sum(lq2_ref[...] * lk2_ref[...], axis=-1, keepdims=True))
           + lambda_init)
    heads = [
        _attn_head(slope_ref[pl.program_id(1) * ATT_HEADS + hd], lam, g_ref,
                   kb_ref.at[hd], db_ref.at[hd],
                   k_ref.at[:, hd * 2 * QK_DIM:(hd + 1) * 2 * QK_DIM],
                   qt_ref.at[hd * 2 * QK_DIM:(hd + 1) * 2 * QK_DIM, :],
                   vt_ref.at[hd * V_DIM:(hd + 1) * V_DIM, :],
                   o_ref.at[:, hd * V_DIM:(hd + 1) * V_DIM],
                   *[sc.at[hd] for sc in scratch], lambda_init=lambda_init)
        for hd in range(ATT_HEADS)]
    n_tiles = heads[0][0]
    for idx in range(min(SCORE_LEAD, n_tiles)):
        for _, emit_scores, _ in heads:
            emit_scores(idx)
    for idx in range(n_tiles):
        for _, emit_scores, softmax_pv in heads:
            if idx + SCORE_LEAD < n_tiles:
                emit_scores(idx + SCORE_LEAD)
            softmax_pv(idx)
    _run_cast_jobs(cast_srcs, cast_dsts)


def _attn_head(slope, lam, g_ref, kb_sc, db_sc, k_ref, qt_ref, vt_ref, o_ref, qs_sc, acc_sc,
               s_sc, *, lambda_init):
    t = ATT_TILE
    nq = SEQ // t
    slope = slope * LOG2E
    row = lax.broadcasted_iota(jnp.int32, (2 * QK_DIM, t), 0)

    for qi in range(nq):
        qt = qt_ref[:, qi * t:(qi + 1) * t]
        zero = jnp.zeros_like(qt)
        qs_sc[qi, 0] = jnp.where(row < QK_DIM, qt, zero)
        qs_sc[qi, 1] = jnp.where(row >= QK_DIM, qt, zero)

    tiles = [(kj, qi, mp) for kj in range(nq) for qi in range(kj, nq) for mp in range(2)]

    def emit_scores(idx):
        kj, qi, mp = tiles[idx]
        k_tile = k_ref[kj * t:(kj + 1) * t, :]
        bias = db_sc[...] if qi == kj else kb_sc[...]
        s = jnp.dot(k_tile, qs_sc[qi, mp], preferred_element_type=F32) + bias
        s_sc[idx % S_SLOTS] = s
        s_max[idx] = jnp.max(s, axis=0, keepdims=True)

    s_max = [None] * len(tiles)
    ones_rows = jnp.ones((ONES_ROWS, t), BF16)
    m = [[None, None] for _ in range(nq)]

    def softmax_pv(idx):
        kj, qi, mp = tiles[idx]
        s = s_sc[idx % S_SLOTS]
        v_ext = jnp.concatenate([vt_ref[:, kj * t:(kj + 1) * t], ones_rows], axis=0)
        shift = slope * jnp.full((1, t), float((qi - kj) * t), F32)
        m_tile = s_max[idx] - shift
        if kj == 0:
            m_new = m_tile
            p = jnp.exp2((s - (m_new + shift)).astype(BF16))
            acc_sc[qi, mp] = jnp.dot(v_ext, p, preferred_element_type=F32)
        else:
            m_new = jnp.maximum(m[qi][mp], m_tile)
            alpha = jnp.exp2(m[qi][mp] - m_new)
            p = jnp.exp2((s - (m_new + shift)).astype(BF16))
            acc_sc[qi, mp] = alpha * acc_sc[qi, mp] + jnp.dot(
                v_ext, p, preferred_element_type=F32)
        m[qi][mp] = m_new
        if qi == kj and mp == 1:
            acc1 = acc_sc[qi, 0]
            acc2 = acc_sc[qi, 1]
            o_t = (acc1[0:V_DIM] / acc1[V_DIM:V_DIM + 1]
                   - lam * (acc2[0:V_DIM] / acc2[V_DIM:V_DIM + 1]))
            ms = jnp.mean(o_t * o_t, axis=0, keepdims=True)
            o = (o_t * lax.rsqrt(ms + SUBLN_EPS)).T
            o = o * g_ref[...] * (1.0 - lambda_init)
            o_ref[qi * t:(qi + 1) * t, :] = o.astype(o_ref.dtype)

    return len(tiles), emit_scores, softmax_pv


def _attention(k, qvt, slopes, lq1, lk1, lq2, lk2, subln_g, lambda_init, cast):
    b = k.shape[0]
    t = ATT_TILE
    smem = pl.BlockSpec(memory_space=pltpu.SMEM)
    lam_spec = _resident((1, QK_DIM))
    hd = ATT_HEADS
    steps = N_HEADS // hd
    jobs = [_cast_job(w, layer, b * steps, lambda bi, h: bi * steps + h) for w, layer in cast]
    key = jnp.arange(t, dtype=F32)[:, None]
    qry = jnp.arange(t, dtype=F32)[None, :]
    slope2 = (slopes * LOG2E)[:, None, None]
    kb = slope2 * jnp.broadcast_to(key, (t, t))
    db = jnp.where((key // CHUNK) <= (qry // CHUNK), slope2 * (qry - jnp.abs(qry - key)),
                   MASK_VALUE)
    table_spec = pl.BlockSpec((hd, t, t), lambda bi, h: (h, 0, 0))
    return pl.pallas_call(
        functools.partial(_attn_kernel, lambda_init=lambda_init, n_cast=len(jobs)),
        grid=(b, steps),
        in_specs=[
            smem, lam_spec, lam_spec, lam_spec, lam_spec, _resident((1, V_DIM)),
            table_spec, table_spec,
            pl.BlockSpec((None, SEQ, hd * 2 * QK_DIM), lambda bi, h: (bi, 0, h)),
            pl.BlockSpec((None, hd * 2 * QK_DIM, SEQ), lambda bi, h: (bi, h, 0)),
            pl.BlockSpec((None, hd * V_DIM, SEQ), lambda bi, h: (bi, steps + h, 0)),
        ] + [job[0] for job in jobs],
        out_specs=[pl.BlockSpec((None, SEQ, hd * V_DIM), lambda bi, h: (bi, 0, h))]
        + [job[1] for job in jobs],
        out_shape=[jax.ShapeDtypeStruct((b, SEQ, N_HEADS * V_DIM), BF16)]
        + [job[2] for job in jobs],
        scratch_shapes=[
            pltpu.VMEM((hd, SEQ // t, 2, 2 * QK_DIM, t), BF16),
            pltpu.VMEM((hd, SEQ // t, 2, V_DIM + ONES_ROWS, t), F32),
            pltpu.VMEM((hd, S_SLOTS, t, t), F32),
        ],
        compiler_params=pltpu.CompilerParams(
            dimension_semantics=("arbitrary", "arbitrary"), vmem_limit_bytes=VMEM_LIMIT),
        name="diff_attention",
    )(slopes, lq1.reshape(1, QK_DIM), lk1.reshape(1, QK_DIM), lq2.reshape(1, QK_DIM),
      lk2.reshape(1, QK_DIM), subln_g.reshape(1, V_DIM), kb, db, k, qvt, qvt,
      *[w for w, _ in cast])


def _causal_conv3(u_sc, halo_ref, u, cw, first, rows):
    hist = halo_ref[...]
    u_sc[0:HALO, :] = jnp.where(first, jnp.zeros_like(hist), hist)
    u_sc[HALO:HALO + rows, :] = u
    halo_ref[...] = u[rows - HALO:rows, :]
    return (u * cw[2:3, :]
            + u_sc[HALO - 1:HALO - 1 + rows, :] * cw[1:2, :]
            + u_sc[HALO - 2:HALO - 2 + rows, :] * cw[0:1, :])


def _ffn_tile(x, first, gpre_ref, gpost_ref, wup_ref, cw_ref, wdn_ref,
              h_sc, u_sc, halo_sc, act_sc):
    rows = x.shape[0]
    c = FFN_CHUNK
    h_sc[...] = _rms(x, gpre_ref[...], NORM_EPS).astype(BF16)
    for j in range(D_FF // c):
        gs = slice(c * j, c * (j + 1))
        vs = slice(D_FF + c * j, D_FF + c * (j + 1))
        h = h_sc[...]
        u = jnp.concatenate(
            [jnp.dot(h, wup_ref[:, gs], preferred_element_type=F32),
             jnp.dot(h, wup_ref[:, vs], preferred_element_type=F32)], axis=1)
        cw = jnp.concatenate([cw_ref[:, gs], cw_ref[:, vs]], axis=1)
        y = _causal_conv3(u_sc, halo_sc.at[j], u, cw, first, rows)
        gate = y[:, 0:c]
        act_sc[:, gs] = (gate * jax.nn.sigmoid(gate) * y[:, c:2 * c]).astype(BF16)
    m = jnp.dot(act_sc[...], wdn_ref[...], preferred_element_type=F32)
    return x + _rms(m, gpost_ref[...], NORM_EPS)


def _ffn_kernel(x_ref, gpre_ref, gpost_ref, wup_ref, cw_ref, wdn_ref, o_ref, *scratch):
    first = (pl.program_id(0) % (SEQ // x_ref.shape[0])) == 0
    o_ref[...] = _ffn_tile(x_ref[...], first, gpre_ref, gpost_ref, wup_ref, cw_ref, wdn_ref,
                           *scratch)


def _oproj_ffn_kernel(a_ref, wo_ref, gmix_ref, x_ref, gpre_ref, gpost_ref, wup_ref, cw_ref,
                      wdn_ref, o_ref, *scratch):
    first = (pl.program_id(0) % (SEQ // x_ref.shape[0])) == 0
    mix = jnp.dot(a_ref[...], wo_ref[...], preferred_element_type=F32)
    x = x_ref[...] + _rms(mix, gmix_ref[...], NORM_EPS)
    o_ref[...] = _ffn_tile(x, first, gpre_ref, gpost_ref, wup_ref, cw_ref, wdn_ref, *scratch)


def _ffn(x2, g_pre, g_post, w_up, conv_w, w_down, attn=None):
    n_rows, d = x2.shape
    c = FFN_CHUNK
    row_spec = pl.BlockSpec((ROW_TILE, d), lambda i: (i, 0))
    in_specs = [
        row_spec,
        _resident((1, d)),
        _resident((1, d)),
        _resident((d, 2 * D_FF)),
        _resident((CONV_WIDTH, 2 * D_FF)),
        _resident((D_FF, d)),
    ]
    args = [x2, g_pre.reshape(1, d), g_post.reshape(1, d), w_up, conv_w, w_down]
    body = _ffn_kernel
    if attn is not None:
        a2, w_o, g_mix = attn
        in_specs = [pl.BlockSpec((ROW_TILE, a2.shape[1]), lambda i: (i, 0)),
                    _resident(w_o.shape), _resident((1, d))] + in_specs
        args = [a2, w_o, g_mix.reshape(1, d)] + args
        body = _oproj_ffn_kernel
    return pl.pallas_call(
        body,
        grid=(n_rows // ROW_TILE,),
        in_specs=in_specs,
        out_specs=row_spec,
        out_shape=jax.ShapeDtypeStruct((n_rows, d), F32),
        scratch_shapes=[
            pltpu.VMEM((ROW_TILE, d), BF16),
            pltpu.VMEM((HALO + ROW_TILE, 2 * c), F32),
            pltpu.VMEM((D_FF // c, HALO, 2 * c), F32),
            pltpu.VMEM((ROW_TILE, D_FF), BF16),
        ],
        compiler_params=pltpu.CompilerParams(
            dimension_semantics=("arbitrary",), vmem_limit_bytes=VMEM_LIMIT),
        name="conv_glu_ffn",
    )(*args)


def _mixer_kernel(x_ref, gpre_ref, gpost_ref, win_ref, cw_ref, wout_ref, o_ref,
                  h_sc, u_sc, halo_sc, y_sc):
    rows = x_ref.shape[0]
    c = MIX_CHUNK
    first = (pl.program_id(0) % (SEQ // rows)) == 0
    x = x_ref[...]
    h_sc[...] = _rms(x, gpre_ref[...], NORM_EPS).astype(BF16)
    for j in range(D_MODEL // c):
        cs = slice(c * j, c * (j + 1))
        h = h_sc[...]
        b_gate, c_gate, hv = (
            jnp.dot(h, win_ref[:, part * D_MODEL + c * j:part * D_MODEL + c * (j + 1)],
                    preferred_element_type=F32) for part in range(3))
        y = _causal_conv3(u_sc, halo_sc.at[j], c_gate * hv, cw_ref[:, cs], first, rows)
        y_sc[:, cs] = (b_gate * y).astype(BF16)
    m = jnp.dot(y_sc[...], wout_ref[...], preferred_element_type=F32)
    o_ref[...] = x + _rms(m, gpost_ref[...], NORM_EPS)


def _mixer(x2, g_pre, g_post, w_in, conv_w, w_out):
    n_rows, d = x2.shape
    c = MIX_CHUNK
    return pl.pallas_call(
        _mixer_kernel,
        grid=(n_rows // ROW_TILE,),
        in_specs=[
            pl.BlockSpec((ROW_TILE, d), lambda i: (i, 0)),
            _resident((1, d)),
            _resident((1, d)),
            _resident((d, 3 * d)),
            _resident((CONV_WIDTH, d)),
            _resident((d, d)),
        ],
        out_specs=pl.BlockSpec((ROW_TILE, d), lambda i: (i, 0)),
        out_shape=jax.ShapeDtypeStruct((n_rows, d), F32),
        scratch_shapes=[
            pltpu.VMEM((ROW_TILE, d), BF16),
            pltpu.VMEM((HALO + ROW_TILE, c), F32),
            pltpu.VMEM((d // c, HALO, c), F32),
            pltpu.VMEM((ROW_TILE, d), BF16),
        ],
        compiler_params=pltpu.CompilerParams(
            dimension_semantics=("arbitrary",), vmem_limit_bytes=VMEM_LIMIT),
        name="conv_mixer",
    )(x2, g_pre.reshape(1, d), g_post.reshape(1, d), w_in, conv_w, w_out)


def _qkv_layout(w):
    d = w.shape[0]
    qk = N_HEADS * QK_DIM
    q1, q2, k1, k2 = (w[:, i * qk:(i + 1) * qk].reshape(d, N_HEADS, QK_DIM) for i in range(4))
    w_k = jnp.concatenate([k1, k2], axis=-1).reshape(d, 2 * qk).astype(BF16)
    w_q = jnp.concatenate([q1, q2], axis=-1).reshape(d, 2 * qk)
    w_qvt = jnp.concatenate([w_q, w[:, 4 * qk:]], axis=1).T.astype(BF16)
    return w_k, w_qvt


def kernel(x, norm_g, attn_w_qkv, attn_w_o, attn_lambda_q1, attn_lambda_k1, attn_lambda_q2,
           attn_lambda_k2, attn_subln_g, conv_w_in, conv_w, conv_w_out, ffn_w_up, ffn_conv_w,
           ffn_w_down):
    b, s, d = x.shape
    assert (s, d) == (SEQ, D_MODEL) and norm_g.shape == (2, 4, d)
    assert attn_w_qkv.shape[0] == 1 and conv_w_in.shape[0] == 1 and ffn_w_up.shape[0] == 2
    slopes = jnp.exp2(-8.0 * jnp.arange(1, N_HEADS + 1, dtype=F32) / N_HEADS)
    x2 = x.reshape(b * s, d)

    g = norm_g[0]
    lambda_init = 0.8 - 0.6 * math.exp(-0.3 * 0)
    w_k, w_qvt = _qkv_layout(attn_w_qkv[0])
    k, qvt, w_o, w_up0, w_dn0 = _qkv_proj(
        x2.reshape(b, s, d), g[0], w_k, w_qvt,
        cast=[(attn_w_o, 0), (ffn_w_up, 0), (ffn_w_down, 0)])
    o, w_in, w_out, w_up1, w_dn1 = _attention(
        k, qvt, slopes, attn_lambda_q1[0], attn_lambda_k1[0], attn_lambda_q2[0],
        attn_lambda_k2[0], attn_subln_g[0], lambda_init,
        cast=[(conv_w_in, 0), (conv_w_out, 0), (ffn_w_up, 1), (ffn_w_down, 1)])
    x2 = _ffn(x2, g[2], g[3], w_up0, ffn_conv_w[0], w_dn0,
              attn=(o.reshape(b * s, d), w_o, g[1]))

    g = norm_g[1]
    x2 = _mixer(x2, g[0], g[1], w_in, conv_w[0], w_out)
    x2 = _ffn(x2, g[2], g[3], w_up1, ffn_conv_w[1], w_dn1)
    return x2.reshape(b, s, d)
```

```python
import functools
import math

import jax
import jax.numpy as jnp
from jax import lax
from jax.experimental import pallas as pl
from jax.experimental.pallas import tpu as pltpu

D_MODEL = 1024
SEQ = 2048
CHUNK = 64
N_HEADS = 8
QK_DIM = 64
V_DIM = 128
QK_COLS = N_HEADS * 4 * QK_DIM
CONV_WIDTH = 3
D_FF = 2816
NORM_EPS = 1e-6
SUBLN_EPS = 1e-5

F32 = jnp.float32
BF16 = jnp.bfloat16

ROW_TILE = 512
ATT_TILE = 256
ATT_HEADS = 2
SCORE_LEAD = 6
S_SLOTS = SCORE_LEAD + 1
BF16_SUBLANES = 16
ONES_ROWS = BF16_SUBLANES
FFN_CHUNK = 256
MIX_CHUNK = 256
HALO = 8
VMEM_LIMIT = 56 * 1024 * 1024
MASK_VALUE = -1e30
LOG2E = math.log2(math.e)

_NT = (((1,), (1,)), ((), ()))


def _rms(x, g, eps):
    ms = jnp.mean(x * x, axis=-1, keepdims=True)
    return x * lax.rsqrt(ms + eps) * g


def _resident(shape):
    nd = len(shape)
    return pl.BlockSpec(shape, lambda *_: (0,) * nd, pipeline_mode=pl.Buffered(1))


def _cast_job(w, layer, n_steps, step_of):
    rows, cols = w.shape[-2:]
    n_blocks = n_steps
    while rows % n_blocks or (rows // n_blocks) % BF16_SUBLANES:
        n_blocks //= 2
    every = n_steps // n_blocks
    rb = rows // n_blocks
    in_spec = pl.BlockSpec((None, rb, cols), lambda *g: (layer, step_of(*g) // every, 0))
    out_spec = pl.BlockSpec((rb, cols), lambda *g: (step_of(*g) // every, 0))
    return in_spec, out_spec, jax.ShapeDtypeStruct((rows, cols), BF16)


def _run_cast_jobs(srcs, dsts):
    for src, dst in zip(srcs, dsts):
        dst[...] = src[...].astype(BF16)


def _qkv_proj_kernel(x_ref, g_ref, wk_ref, wqvt_ref, *refs, n_cast):
    cast_srcs, (k_ref, qvt_ref), cast_dsts = refs[:n_cast], refs[n_cast:n_cast + 2], refs[n_cast + 2:]
    h = _rms(x_ref[...], g_ref[...], NORM_EPS).astype(BF16)
    k_ref[...] = jnp.dot(h, wk_ref[...], preferred_element_type=F32).astype(BF16)
    qvt = lax.dot_general(wqvt_ref[...], h, _NT, preferred_element_type=F32)
    n_q = N_HEADS * 2 * QK_DIM
    qvt_ref[0:n_q, :] = (qvt[0:n_q] * (LOG2E * QK_DIM ** -0.5)).astype(BF16)
    qvt_ref[n_q:, :] = qvt[n_q:].astype(BF16)
    _run_cast_jobs(cast_srcs, cast_dsts)


def _qkv_proj(x3, g, w_k, w_qvt, cast):
    b, s, d = x3.shape
    tiles = s // ROW_TILE
    n_k, n_qv = w_k.shape[1], w_qvt.shape[0]
    jobs = [_cast_job(w, layer, b * tiles, lambda bi, i: bi * tiles + i) for w, layer in cast]
    return pl.pallas_call(
        functools.partial(_qkv_proj_kernel, n_cast=len(jobs)),
        grid=(b, tiles),
        in_specs=[
            pl.BlockSpec((None, ROW_TILE, d), lambda bi, i: (bi, i, 0)),
            _resident((1, d)),
            _resident((d, n_k)),
            _resident((n_qv, d)),
        ] + [job[0] for job in jobs],
        out_specs=[
            pl.BlockSpec((None, ROW_TILE, n_k), lambda bi, i: (bi, i, 0)),
            pl.BlockSpec((None, n_qv, ROW_TILE), lambda bi, i: (bi, 0, i)),
        ] + [job[1] for job in jobs],
        out_shape=[
            jax.ShapeDtypeStruct((b, s, n_k), BF16),
            jax.ShapeDtypeStruct((b, n_qv, s), BF16),
        ] + [job[2] for job in jobs],
        compiler_params=pltpu.CompilerParams(
            dimension_semantics=("arbitrary", "arbitrary"), vmem_limit_bytes=VMEM_LIMIT),
        name="qkv_proj",
    )(x3, g.reshape(1, d), w_k, w_qvt, *[w for w, _ in cast])


def _attn_kernel(slope_ref, lq1_ref, lk1_ref, lq2_ref, lk2_ref, g_ref, kb_ref, db_ref, k_ref,
                 qt_ref, vt_ref, *refs, lambda_init, n_cast):
    cast_srcs, o_ref, cast_dsts = refs[:n_cast], refs[n_cast], refs[n_cast + 1:2 * n_cast + 1]
    scratch = refs[2 * n_cast + 1:]
    lam = (jnp.exp(jnp.sum(lq1_ref[...] * lk1_ref[...], axis=-1, keepdims=True))
           - jnp.exp(jnp.sum(lq2_ref[...] * lk2_ref[...], axis=-1, keepdims=True))
           + lambda_init)
    heads = [
        _attn_head(slope_ref[pl.program_id(1) * ATT_HEADS + hd], lam, g_ref,
                   kb_ref.at[hd], db_ref.at[hd],
                   k_ref.at[:, hd * 2 * QK_DIM:(hd + 1) * 2 * QK_DIM],
                   qt_ref.at[hd * 2 * QK_DIM:(hd + 1) * 2 * QK_DIM, :],
                   vt_ref.at[hd * V_DIM:(hd + 1) * V_DIM, :],
                   o_ref.at[:, hd * V_DIM:(hd + 1) * V_DIM],
                   *[sc.at[hd] for sc in scratch], lambda_init=lambda_init)
        for hd in range(ATT_HEADS)]
    n_tiles = heads[0][0]
    for idx in range(min(SCORE_LEAD, n_tiles)):
        for _, emit_scores, _ in heads:
            emit_scores(idx)
    for idx in range(n_tiles):
        for _, emit_scores, softmax_pv in heads:
            if idx + SCORE_LEAD < n_tiles:
                emit_scores(idx + SCORE_LEAD)
            softmax_pv(idx)
    _run_cast_jobs(cast_srcs, cast_dsts)


def _attn_head(slope, lam, g_ref, kb_sc, db_sc, k_ref, qt_ref, vt_ref, o_ref, qs_sc, acc_sc,
               s_sc, *, lambda_init):
    t = ATT_TILE
    nq = SEQ // t
    slope = slope * LOG2E
    row = lax.broadcasted_iota(jnp.int32, (2 * QK_DIM, t), 0)

    for qi in range(nq):
        qt = qt_ref[:, qi * t:(qi + 1) * t]
        zero = jnp.zeros_like(qt)
        qs_sc[qi, 0] = jnp.where(row < QK_DIM, qt, zero)
        qs_sc[qi, 1] = jnp.where(row >= QK_DIM, qt, zero)

    tiles = [(kj, qi, mp) for kj in range(nq) for qi in range(kj, nq) for mp in range(2)]

    def emit_scores(idx):
        kj, qi, mp = tiles[idx]
        k_tile = k_ref[kj * t:(kj + 1) * t, :]
        bias = db_sc[...] if qi == kj else kb_sc[...]
        s = jnp.dot(k_tile, qs_sc[qi, mp], preferred_element_type=F32) + bias
        s_sc[idx % S_SLOTS] = s
        s_max[idx] = jnp.max(s, axis=0, keepdims=True)

    s_max = [None] * len(tiles)
    ones_rows = jnp.ones((ONES_ROWS, t), BF16)
    m = [[None, None] for _ in range(nq)]

    def softmax_pv(idx):
        kj, qi, mp = tiles[idx]
        s = s_sc[idx % S_SLOTS]
        v_ext = jnp.concatenate([vt_ref[:, kj * t:(kj + 1) * t], ones_rows], axis=0)
        shift = slope * jnp.full((1, t), float((qi - kj) * t), F32)
        m_tile = s_max[idx] - shift
        if kj == 0:
            m_new = m_tile
            p = jnp.exp2((s - (m_new + shift)).astype(BF16))
            acc_sc[qi, mp] = jnp.dot(v_ext, p, preferred_element_type=F32)
        else:
            m_new = jnp.maximum(m[qi][mp], m_tile)
            alpha = jnp.exp2(m[qi][mp] - m_new)
            p = jnp.exp2((s - (m_new + shift)).astype(BF16))
            acc_sc[qi, mp] = alpha * acc_sc[qi, mp] + jnp.dot(
                v_ext, p, preferred_element_type=F32)
        m[qi][mp] = m_new
        if qi == kj and mp == 1:
            acc1 = acc_sc[qi, 0]
            acc2 = acc_sc[qi, 1]
            o_t = (acc1[0:V_DIM] / acc1[V_DIM:V_DIM + 1]
                   - lam * (acc2[0:V_DIM] / acc2[V_DIM:V_DIM + 1]))
            ms = jnp.mean(o_t * o_t, axis=0, keepdims=True)
            o = (o_t * lax.rsqrt(ms + SUBLN_EPS)).T
            o = o * g_ref[...] * (1.0 - lambda_init)
            o_ref[qi * t:(qi + 1) * t, :] = o.astype(o_ref.dtype)

    return len(tiles), emit_scores, softmax_pv


def _attention(k, qvt, slopes, lq1, lk1, lq2, lk2, subln_g, lambda_init, cast):
    b = k.shape[0]
    t = ATT_TILE
    smem = pl.BlockSpec(memory_space=pltpu.SMEM)
    lam_spec = _resident((1, QK_DIM))
    hd = ATT_HEADS
    steps = N_HEADS // hd
    jobs = [_cast_job(w, layer, b * steps, lambda bi, h: bi * steps + h) for w, layer in cast]
    key = jnp.arange(t, dtype=F32)[:, None]
    qry = jnp.arange(t, dtype=F32)[None, :]
    slope2 = (slopes * LOG2E)[:, None, None]
    kb = slope2 * jnp.broadcast_to(key, (t, t))
    db = jnp.where((key // CHUNK) <= (qry // CHUNK), slope2 * (qry - jnp.abs(qry - key)),
                   MASK_VALUE)
    table_spec = pl.BlockSpec((hd, t, t), lambda bi, h: (h, 0, 0))
    return pl.pallas_call(
        functools.partial(_attn_kernel, lambda_init=lambda_init, n_cast=len(jobs)),
        grid=(b, steps),
        in_specs=[
            smem, lam_spec, lam_spec, lam_spec, lam_spec, _resident((1, V_DIM)),
            table_spec, table_spec,
            pl.BlockSpec((None, SEQ, hd * 2 * QK_DIM), lambda bi, h: (bi, 0, h)),
            pl.BlockSpec((None, hd * 2 * QK_DIM, SEQ), lambda bi, h: (bi, h, 0)),
            pl.BlockSpec((None, hd * V_DIM, SEQ), lambda bi, h: (bi, steps + h, 0)),
        ] + [job[0] for job in jobs],
        out_specs=[pl.BlockSpec((None, SEQ, hd * V_DIM), lambda bi, h: (bi, 0, h))]
        + [job[1] for job in jobs],
        out_shape=[jax.ShapeDtypeStruct((b, SEQ, N_HEADS * V_DIM), BF16)]
        + [job[2] for job in jobs],
        scratch_shapes=[
            pltpu.VMEM((hd, SEQ // t, 2, 2 * QK_DIM, t), BF16),
            pltpu.VMEM((hd, SEQ // t, 2, V_DIM + ONES_ROWS, t), F32),
            pltpu.VMEM((hd, S_SLOTS, t, t), F32),
        ],
        compiler_params=pltpu.CompilerParams(
            dimension_semantics=("arbitrary", "arbitrary"), vmem_limit_bytes=VMEM_LIMIT),
        name="diff_attention",
    )(slopes, lq1.reshape(1, QK_DIM), lk1.reshape(1, QK_DIM), lq2.reshape(1, QK_DIM),
      lk2.reshape(1, QK_DIM), subln_g.reshape(1, V_DIM), kb, db, k, qvt, qvt,
      *[w for w, _ in cast])


def _causal_conv3(u_sc, halo_ref, u, cw, first, rows):
    hist = halo_ref[...]
    u_sc[0:HALO, :] = jnp.where(first, jnp.zeros_like(hist), hist)
    u_sc[HALO:HALO + rows, :] = u
    halo_ref[...] = u[rows - HALO:rows, :]
    return (u * cw[2:3, :]
            + u_sc[HALO - 1:HALO - 1 + rows, :] * cw[1:2, :]
            + u_sc[HALO - 2:HALO - 2 + rows, :] * cw[0:1, :])


def _ffn_tile(x, first, gpre_ref, gpost_ref, wup_ref, cw_ref, wdn_ref,
              h_sc, u_sc, halo_sc, act_sc):
    rows = x.shape[0]
    c = FFN_CHUNK
    h_sc[...] = _rms(x, gpre_ref[...], NORM_EPS).astype(BF16)
    for j in range(D_FF // c):
        gs = slice(c * j, c * (j + 1))
        vs = slice(D_FF + c * j, D_FF + c * (j + 1))
        h = h_sc[...]
        u = jnp.concatenate(
            [jnp.dot(h, wup_ref[:, gs], preferred_element_type=F32),
             jnp.dot(h, wup_ref[:, vs], preferred_element_type=F32)], axis=1)
        cw = jnp.concatenate([cw_ref[:, gs], cw_ref[:, vs]], axis=1)
        y = _causal_conv3(u_sc, halo_sc.at[j], u, cw, first, rows)
        gate = y[:, 0:c]
        act_sc[:, gs] = (gate * jax.nn.sigmoid(gate) * y[:, c:2 * c]).astype(BF16)
    m = jnp.dot(act_sc[...], wdn_ref[...], preferred_element_type=F32)
    return x + _rms(m, gpost_ref[...], NORM_EPS)


def _ffn_kernel(x_ref, gpre_ref, gpost_ref, wup_ref, cw_ref, wdn_ref, o_ref, *scratch):
    first = (pl.program_id(0) % (SEQ // x_ref.shape[0])) == 0
    o_ref[...] = _ffn_tile(x_ref[...], first, gpre_ref, gpost_ref, wup_ref, cw_ref, wdn_ref,
                           *scratch)


def _oproj_ffn_kernel(a_ref, wo_ref, gmix_ref, x_ref, gpre_ref, gpost_ref, wup_ref, cw_ref,
                      wdn_ref, o_ref, *scratch):
    first = (pl.program_id(0) % (SEQ // x_ref.shape[0])) == 0
    mix = jnp.dot(a_ref[...], wo_ref[...], preferred_element_type=F32)
    x = x_ref[...] + _rms(mix, gmix_ref[...], NORM_EPS)
    o_ref[...] = _ffn_tile(x, first, gpre_ref, gpost_ref, wup_ref, cw_ref, wdn_ref, *scratch)


def _ffn(x2, g_pre, g_post, w_up, conv_w, w_down, attn=None):
    n_rows, d = x2.shape
    c = FFN_CHUNK
    row_spec = pl.BlockSpec((ROW_TILE, d), lambda i: (i, 0))
    in_specs = [
        row_spec,
        _resident((1, d)),
        _resident((1, d)),
        _resident((d, 2 * D_FF)),
        _resident((CONV_WIDTH, 2 * D_FF)),
        _resident((D_FF, d)),
    ]
    args = [x2, g_pre.reshape(1, d), g_post.reshape(1, d), w_up, conv_w, w_down]
    body = _ffn_kernel
    if attn is not None:
        a2, w_o, g_mix = attn
        in_specs = [pl.BlockSpec((ROW_TILE, a2.shape[1]), lambda i: (i, 0)),
                    _resident(w_o.shape), _resident((1, d))] + in_specs
        args = [a2, w_o, g_mix.reshape(1, d)] + args
        body = _oproj_ffn_kernel
    return pl.pallas_call(
        body,
        grid=(n_rows // ROW_TILE,),
        in_specs=in_specs,
        out_specs=row_spec,
        out_shape=jax.ShapeDtypeStruct((n_rows, d), F32),
        scratch_shapes=[
            pltpu.VMEM((ROW_TILE, d), BF16),
            pltpu.VMEM((HALO + ROW_TILE, 2 * c), F32),
            pltpu.VMEM((D_FF // c, HALO, 2 * c), F32),
            pltpu.VMEM((ROW_TILE, D_FF), BF16),
        ],
        compiler_params=pltpu.CompilerParams(
            dimension_semantics=("arbitrary",), vmem_limit_bytes=VMEM_LIMIT),
        name="conv_glu_ffn",
    )(*args)


def _mixer_kernel(x_ref, gpre_ref, gpost_ref, win_ref, cw_ref, wout_ref, o_ref,
                  h_sc, u_sc, halo_sc, y_sc):
    rows = x_ref.shape[0]
    c = MIX_CHUNK
    first = (pl.program_id(0) % (SEQ // rows)) == 0
    x = x_ref[...]
    h_sc[...] = _rms(x, gpre_ref[...], NORM_EPS).astype(BF16)
    for j in range(D_MODEL // c):
        cs = slice(c * j, c * (j + 1))
        h = h_sc[...]
        b_gate, c_gate, hv = (
            jnp.dot(h, win_ref[:, part * D_MODEL + c * j:part * D_MODEL + c * (j + 1)],
                    preferred_element_type=F32) for part in range(3))
        y = _causal_conv3(u_sc, halo_sc.at[j], c_gate * hv, cw_ref[:, cs], first, rows)
        y_sc[:, cs] = (b_gate * y).astype(BF16)
    m = jnp.dot(y_sc[...], wout_ref[...], preferred_element_type=F32)
    o_ref[...] = x + _rms(m, gpost_ref[...], NORM_EPS)


def _mixer(x2, g_pre, g_post, w_in, conv_w, w_out):
    n_rows, d = x2.shape
    c = MIX_CHUNK
    return pl.pallas_call(
        _mixer_kernel,
        grid=(n_rows // ROW_TILE,),
        in_specs=[
            pl.BlockSpec((ROW_TILE, d), lambda i: (i, 0)),
            _resident((1, d)),
            _resident((1, d)),
            _resident((d, 3 * d)),
            _resident((CONV_WIDTH, d)),
            _resident((d, d)),
        ],
        out_specs=pl.BlockSpec((ROW_TILE, d), lambda i: (i, 0)),
        out_shape=jax.ShapeDtypeStruct((n_rows, d), F32),
        scratch_shapes=[
            pltpu.VMEM((ROW_TILE, d), BF16),
            pltpu.VMEM((HALO + ROW_TILE, c), F32),
            pltpu.VMEM((d // c, HALO, c), F32),
            pltpu.VMEM((ROW_TILE, d), BF16),
        ],
        compiler_params=pltpu.CompilerParams(
            dimension_semantics=("arbitrary",), vmem_limit_bytes=VMEM_LIMIT),
        name="conv_mixer",
    )(x2, g_pre.reshape(1, d), g_post.reshape(1, d), w_in, conv_w, w_out)


def _qkv_layout(w):
    d = w.shape[0]
    qk = N_HEADS * QK_DIM
    q1, q2, k1, k2 = (w[:, i * qk:(i + 1) * qk].reshape(d, N_HEADS, QK_DIM) for i in range(4))
    w_k = jnp.concatenate([k1, k2], axis=-1).reshape(d, 2 * qk).astype(BF16)
    w_q = jnp.concatenate([q1, q2], axis=-1).reshape(d, 2 * qk)
    w_qvt = jnp.concatenate([w_q, w[:, 4 * qk:]], axis=1).T.astype(BF16)
    return w_k, w_qvt


def kernel(x, norm_g, attn_w_qkv, attn_w_o, attn_lambda_q1, attn_lambda_k1, attn_lambda_q2,
           attn_lambda_k2, attn_subln_g, conv_w_in, conv_w, conv_w_out, ffn_w_up, ffn_conv_w,
           ffn_w_down):
    b, s, d = x.shape
    assert (s, d) == (SEQ, D_MODEL) and norm_g.shape == (2, 4, d)
    assert attn_w_qkv.shape[0] == 1 and conv_w_in.shape[0] == 1 and ffn_w_up.shape[0] == 2
    slopes = jnp.exp2(-8.0 * jnp.arange(1, N_HEADS + 1, dtype=F32) / N_HEADS)
    x2 = x.reshape(b * s, d)

    g = norm_g[0]
    lambda_init = 0.8 - 0.6 * math.exp(-0.3 * 0)
    w_k, w_qvt = _qkv_layout(attn_w_qkv[0])
    k, qvt, w_o, w_up0, w_dn0 = _qkv_proj(
        x2.reshape(b, s, d), g[0], w_k, w_qvt,
        cast=[(attn_w_o, 0), (ffn_w_up, 0), (ffn_w_down, 0)])
    o, w_in, w_out, w_up1, w_dn1 = _attention(
        k, qvt, slopes, attn_lambda_q1[0], attn_lambda_k1[0], attn_lambda_q2[0],
        attn_lambda_k2[0], attn_subln_g[0], lambda_init,
        cast=[(conv_w_in, 0), (conv_w_out, 0), (ffn_w_up, 1), (ffn_w_down, 1)])
    x2 = _ffn(x2, g[2], g[3], w_up0, ffn_conv_w[0], w_dn0,
              attn=(o.reshape(b * s, d), w_o, g[1]))

    g = norm_g[1]
    x2 = _mixer(x2, g[0], g[1], w_in, conv_w[0], w_out)
    x2 = _ffn(x2, g[2], g[3], w_up1, ffn_conv_w[1], w_dn1)
    return x2.reshape(b, s, d)
```
